```python
import math
import jax, jax.numpy as jnp
from jax import lax
import numpy as np

D_MODEL = 2048
BATCH = 8
SEQ = 2048
DEPTH = 4

EPS = 1e-6
MLA_HEADS = 16
MLA_Q_RANK = 512
MLA_KV_RANK = 256
MLA_NOPE = 128
MLA_ROPE = 64
MLA_V = 128
ROPE_THETA = 10000.0
Q_BLOCK = 128
DIL_PAIRS = ((128, 1), (512, 4), (2048, 16))
DIL_GROUPS = 3
DIL_HEADS_PER_GROUP = 8
DIL_HEADS = DIL_GROUPS * DIL_HEADS_PER_GROUP
DIL_HEAD_DIM = 128
DIL_BLOCK = 128
REL_BUCKETS = 32
REL_MAX_DIST = 2048
D_FF = 5632
N_EXPERTS = 8
TOP_K = 2
N_DENSE = (DEPTH + 1) // 2
N_MOE = DEPTH // 2
COL_QA = MLA_Q_RANK
COL_KVA = MLA_KV_RANK + MLA_ROPE
COL_DIL = 3 * DIL_HEADS * DIL_HEAD_DIM
COL_GATE = 2 * D_MODEL
OFF_KVA = COL_QA
OFF_DIL = OFF_KVA + COL_KVA
OFF_GATE = OFF_DIL + COL_DIL
D_IN = OFF_GATE + COL_GATE
MLA_OUT = MLA_HEADS * MLA_V
DIL_OUT = DIL_HEADS_PER_GROUP * DIL_HEAD_DIM

kernel_name = 'hybrid_mla_dilated_moe_trunk'


def rmsnorm(x, g):
    xf = x.astype(jnp.float32)
    y = xf * lax.rsqrt(jnp.mean(xf * xf, axis=-1, keepdims=True) + EPS)
    return (y * g.astype(jnp.float32)).astype(x.dtype)


def rope(x, pos):
    half = x.shape[-1] // 2
    inv = ROPE_THETA ** (-jnp.arange(half, dtype=jnp.float32) / half)
    ang = pos.astype(jnp.float32)[:, None] * inv[None, :]
    cos = jnp.cos(ang)[:, None, :]
    sin = jnp.sin(ang)[:, None, :]
    xf = x.astype(jnp.float32)
    x1, x2 = xf[..., :half], xf[..., half:]
    return jnp.concatenate([x1 * cos - x2 * sin, x2 * cos + x1 * sin], axis=-1).astype(x.dtype)


def mla(qa, kva, q_norm, w_uq, kv_norm, w_ukv):
    B, S, _ = qa.shape
    pos = jnp.arange(S)
    c_q = rmsnorm(qa, q_norm)
    q = (c_q @ w_uq).reshape(B, S, MLA_HEADS, MLA_NOPE + MLA_ROPE)
    q_nope = q[..., :MLA_NOPE]
    q_pe = rope(q[..., MLA_NOPE:], pos)
    c_kv = rmsnorm(kva[..., :MLA_KV_RANK], kv_norm)
    k_pe = rope(kva[..., None, MLA_KV_RANK:], pos)[:, :, 0]
    kv = (c_kv @ w_ukv).reshape(B, S, MLA_HEADS, MLA_NOPE + MLA_V)
    k_nope, v = kv[..., :MLA_NOPE], kv[..., MLA_NOPE:]
    scale = (MLA_NOPE + MLA_ROPE) ** -0.5
    nb = S // Q_BLOCK
    qn_b = q_nope.reshape(B, nb, Q_BLOCK, MLA_HEADS, MLA_NOPE).transpose(1, 0, 2, 3, 4)
    qp_b = q_pe.reshape(B, nb, Q_BLOCK, MLA_HEADS, MLA_ROPE).transpose(1, 0, 2, 3, 4)
    key_pos = jnp.arange(S)

    def block(args):
        i, qn, qp = args
        s = (jnp.einsum('bqhd,bkhd->bhqk', qn, k_nope)
             + jnp.einsum('bqhr,bkr->bhqk', qp, k_pe)).astype(jnp.float32) * scale
        q_pos = i * Q_BLOCK + jnp.arange(Q_BLOCK)
        causal = key_pos[None, :] <= q_pos[:, None]
        s = jnp.where(causal[None, None], s, -jnp.inf)
        p = jax.nn.softmax(s, axis=-1).astype(v.dtype)
        return jnp.einsum('bhqk,bkhd->bqhd', p, v)

    o = lax.map(block, (jnp.arange(nb), qn_b, qp_b))
    return o.transpose(1, 0, 2, 3, 4).reshape(B, S, MLA_OUT)


def t5_bucket(dist):
    max_exact = REL_BUCKETS // 2
    is_small = dist < max_exact
    large = max_exact + (jnp.log(jnp.maximum(dist, 1).astype(jnp.float32) / max_exact)
                         / math.log(REL_MAX_DIST / max_exact) * (REL_BUCKETS - max_exact)).astype(jnp.int32)
    large = jnp.minimum(large, REL_BUCKETS - 1)
    return jnp.where(is_small, dist, large)


def dilated_group(q, k, v, rel_bias, window, dilation):
    B, S, Hg, dh = q.shape
    L = S // dilation
    nb = -(-L // DIL_BLOCK)
    Lp = nb * DIL_BLOCK
    span = window // dilation

    def classes(t):
        t = t.reshape(B, L, dilation, Hg, dh).transpose(0, 3, 2, 1, 4)
        return jnp.pad(t, ((0, 0), (0, 0), (0, 0), (0, Lp - L), (0, 0)))

    front = ((0, 0), (0, 0), (0, 0), (DIL_BLOCK, 0), (0, 0))
    qc = classes(q).reshape(B, Hg, dilation, nb, DIL_BLOCK, dh)
    kc = jnp.pad(classes(k), front).reshape(B, Hg, dilation, nb + 1, DIL_BLOCK, dh)
    vc = jnp.pad(classes(v), front).reshape(B, Hg, dilation, nb + 1, DIL_BLOCK, dh)
    kb = jnp.concatenate([kc[:, :, :, :-1], kc[:, :, :, 1:]], axis=4)
    vb = jnp.concatenate([vc[:, :, :, :-1], vc[:, :, :, 1:]], axis=4)
    s = jnp.einsum('bhrnqd,bhrnkd->bhrnqk', qc, kb).astype(jnp.float32) * (dh ** -0.5)
    iq = jnp.arange(DIL_BLOCK)[:, None]
    jk = jnp.arange(2 * DIL_BLOCK)[None, :]
    dist_c = DIL_BLOCK + iq - jk
    key_idx = jnp.arange(nb)[:, None, None] * DIL_BLOCK + jk[None] - DIL_BLOCK
    valid = (dist_c >= 0)[None] & (dist_c <= span)[None] & (key_idx >= 0)
    bucket = t5_bucket(jnp.clip(dist_c, 0, None) * dilation)
    bias = rel_bias[bucket].astype(jnp.float32).transpose(2, 0, 1)
    s = s + bias[None, :, None, None]
    s = jnp.where(valid[None, None, None], s, -jnp.inf)
    m = jnp.max(s, axis=-1, keepdims=True)
    p = jnp.exp(s - m)
    l = jnp.sum(p, axis=-1, keepdims=True)
    o = jnp.einsum('bhrnqk,bhrnkd->bhrnqd', p.astype(v.dtype), vb).astype(jnp.float32) / l
    lse = (m + jnp.log(l))[..., 0]
    o = o.reshape(B, Hg, dilation, Lp, dh)[:, :, :, :L].transpose(0, 3, 2, 1, 4).reshape(B, S, Hg, dh)
    lse = lse.reshape(B, Hg, dilation, Lp)[..., :L].transpose(0, 3, 2, 1).reshape(B, S, Hg)
    return o, lse


def dilated_attention(dil, rel_bias):
    B, S, _ = dil.shape
    qkv = dil.reshape(B, S, 3, DIL_HEADS, DIL_HEAD_DIM)
    q, k, v = qkv[:, :, 0], qkv[:, :, 1], qkv[:, :, 2]
    outs, lses = [], []
    for g, (window, dilation) in enumerate(DIL_PAIRS):
        sl = slice(g * DIL_HEADS_PER_GROUP, (g + 1) * DIL_HEADS_PER_GROUP)
        o, lse = dilated_group(q[:, :, sl], k[:, :, sl], v[:, :, sl], rel_bias[:, sl], window, dilation)
        outs.append(o)
        lses.append(lse)
    alpha = jax.nn.softmax(jnp.stack(lses, axis=0), axis=0)
    o = jnp.sum(alpha[..., None] * jnp.stack(outs, axis=0), axis=0)
    return o.astype(dil.dtype).reshape(B, S, DIL_OUT)


def swiglu(h, w_gate, w_up, w_down):
    return (jax.nn.silu(h @ w_gate) * (h @ w_up)) @ w_down


def moe(h, w_router, w_gate, w_up, w_down):
    logits = (h @ w_router).astype(jnp.float32)
    top_val, top_idx = lax.top_k(logits, TOP_K)
    wts = jax.nn.softmax(top_val, axis=-1)
    gate = jnp.sum(jax.nn.one_hot(top_idx, N_EXPERTS, dtype=jnp.float32) * wts[..., None], axis=-2)
    out = jnp.zeros_like(h)
    for e in range(N_EXPERTS):
        out = out + gate[..., e:e + 1].astype(h.dtype) * swiglu(h, w_gate[e], w_up[e], w_down[e])
    return out


def setup_inputs(seed: int = 0) -> dict:
    key = jax.random.key(seed)
    ks = jax.random.split(key, 24)

    def nrm(k, shape, fan_in):
        return jax.random.normal(k, shape, jnp.float32) * (fan_in ** -0.5)

    def gain(k, shape):
        return 1.0 + 0.05 * jax.random.normal(k, shape, jnp.float32)

    return {
        'x': jax.random.normal(ks[0], (BATCH, SEQ, D_MODEL), jnp.float32),
        'mix_norm': gain(ks[1], (DEPTH, D_MODEL)),
        'w_in': nrm(ks[2], (DEPTH, D_MODEL, D_IN), D_MODEL),
        'q_norm': gain(ks[3], (DEPTH, MLA_Q_RANK)),
        'w_uq': nrm(ks[4], (DEPTH, MLA_Q_RANK, MLA_HEADS * (MLA_NOPE + MLA_ROPE)), MLA_Q_RANK),
        'kv_norm': gain(ks[5], (DEPTH, MLA_KV_RANK)),
        'w_ukv': nrm(ks[6], (DEPTH, MLA_KV_RANK, MLA_HEADS * (MLA_NOPE + MLA_V)), MLA_KV_RANK),
        'w_o_mla': nrm(ks[7], (DEPTH, MLA_OUT, D_MODEL), MLA_OUT),
        'w_o_dil': nrm(ks[8], (DEPTH, DIL_OUT, D_MODEL), DIL_OUT),
        'w_out': nrm(ks[9], (DEPTH, D_MODEL, D_MODEL), D_MODEL),
        'rel_bias': 0.2 * jax.random.normal(ks[10], (REL_BUCKETS, DIL_HEADS), jnp.float32),
        'ffn_norm': gain(ks[11], (DEPTH, D_MODEL)),
        'w_ffn_gate': nrm(ks[12], (N_DENSE, D_MODEL, D_FF), D_MODEL),
        'w_ffn_up': nrm(ks[13], (N_DENSE, D_MODEL, D_FF), D_MODEL),
        'w_ffn_down': nrm(ks[14], (N_DENSE, D_FF, D_MODEL), D_FF),
        'w_router': nrm(ks[15], (N_MOE, D_MODEL, N_EXPERTS), D_MODEL),
        'w_exp_gate': nrm(ks[16], (N_MOE, N_EXPERTS, D_MODEL, D_FF), D_MODEL),
        'w_exp_up': nrm(ks[17], (N_MOE, N_EXPERTS, D_MODEL, D_FF), D_MODEL),
        'w_exp_down': nrm(ks[18], (N_MOE, N_EXPERTS, D_FF, D_MODEL), D_FF),
        'final_norm': gain(ks[19], (D_MODEL,)),
    }


def reference(x, mix_norm, w_in, q_norm, w_uq, kv_norm, w_ukv, w_o_mla, w_o_dil, w_out, rel_bias,
              ffn_norm, w_ffn_gate, w_ffn_up, w_ffn_down, w_router, w_exp_gate, w_exp_up, w_exp_down,
              final_norm):
    for layer in range(DEPTH):
        h = rmsnorm(x, mix_norm[layer])
        z = h @ w_in[layer]
        qa = z[..., :OFF_KVA]
        kva = z[..., OFF_KVA:OFF_DIL]
        dil = z[..., OFF_DIL:OFF_GATE]
        gates = jax.nn.sigmoid(z[..., OFF_GATE:])
        y_a = mla(qa, kva, q_norm[layer], w_uq[layer], kv_norm[layer], w_ukv[layer]) @ w_o_mla[layer]
        y_b = dilated_attention(dil, rel_bias) @ w_o_dil[layer]
        merged = gates[..., :D_MODEL] * y_a + gates[..., D_MODEL:] * y_b
        x = x + merged @ w_out[layer]
        h = rmsnorm(x, ffn_norm[layer])
        if layer % 2 == 0:
            i = layer // 2
            x = x + swiglu(h, w_ffn_gate[i], w_ffn_up[i], w_ffn_down[i])
        else:
            i = layer // 2
            x = x + moe(h, w_router[i], w_exp_gate[i], w_exp_up[i], w_exp_down[i])
    return rmsnorm(x, final_norm)
```

```python
import functools
import math

import numpy as np
import jax
import jax.numpy as jnp
from jax import lax
from jax.experimental import pallas as pl
from jax.experimental.pallas import tpu as pltpu

F32 = jnp.float32
BF16 = jnp.bfloat16

EPS = 1e-6
NEG = -1e30

MLA_HEADS = 16
MLA_Q_RANK = 512
MLA_KV_RANK = 256
MLA_NOPE = 128
MLA_ROPE = 64
MLA_V = 128
ROPE_THETA = 10000.0
DIL_PAIRS = ((128, 1), (512, 4), (2048, 16))
DIL_GROUPS = 3
DIL_HPG = 8
DIL_HEADS = DIL_GROUPS * DIL_HPG
DIL_DH = 128
DIL_BLOCK = 128
REL_BUCKETS = 32
REL_MAX_DIST = 2048
N_EXPERTS = 8
TOP_K = 2

LANE = 128
MLA_QK = 256
VMEM_LIMIT = 56 * 2**20

TM = 1024
TM_DOWN = 512
TN = 512
TM_MOE = 512
TQ = 512
GATHER_ROWS = 256
NORM_CHUNK = 256


def _cparams(*sem):
    return pltpu.CompilerParams(dimension_semantics=sem, vmem_limit_bytes=VMEM_LIMIT)


def _rms_rows(x, g):
    ms = jnp.mean(x * x, axis=-1, keepdims=True)
    return x * lax.rsqrt(ms + EPS) * g


def _norm_into(x_ref, g_ref, h_scr):
    rows = x_ref.shape[0]
    chunk = min(NORM_CHUNK, rows)
    for r in range(0, rows, chunk):
        x = x_ref[r:r + chunk, :].astype(F32)
        h_scr[r:r + chunk, :] = _rms_rows(x, g_ref[...]).astype(h_scr.dtype)


def _sigmoid(x):
    return 1.0 / (1.0 + jnp.exp(-x))


def _norm_mm_body(x_ref, g_ref, w_ref, o_ref, h_scr, *, sig_from):
    j = pl.program_id(1)

    @pl.when(j == 0)
    def _():
        _norm_into(x_ref, g_ref, h_scr)

    acc = jnp.dot(h_scr[...], w_ref[...], preferred_element_type=F32)
    if sig_from is None:
        o_ref[...] = acc.astype(o_ref.dtype)
    else:
        @pl.when(j < sig_from)
        def _():
            o_ref[...] = acc.astype(o_ref.dtype)

        @pl.when(j >= sig_from)
        def _():
            o_ref[...] = _sigmoid(acc).astype(o_ref.dtype)


def norm_matmul(x, gain, w, *, tm, tn, sig_from=None):
    t, k = x.shape
    n = w.shape[1]
    return pl.pallas_call(
        functools.partial(_norm_mm_body, sig_from=sig_from),
        grid=(t // tm, n // tn),
        in_specs=[
            pl.BlockSpec((tm, k), lambda i, j: (i, 0)),
            pl.BlockSpec((1, k), lambda i, j: (0, 0)),
            pl.BlockSpec((k, tn), lambda i, j: (0, j)),
        ],
        out_specs=pl.BlockSpec((tm, tn), lambda i, j: (i, j)),
        out_shape=jax.ShapeDtypeStruct((t, n), BF16),
        scratch_shapes=[pltpu.VMEM((tm, k), BF16)],
        compiler_params=_cparams("parallel", "arbitrary"),
        name="norm_matmul",
    )(x, gain.reshape(1, k), w)


def _rope128(pe, c_ref, sa_ref, sb_ref):
    return (pe * c_ref[...]
            + pltpu.roll(pe, 96, 1) * sa_ref[...]
            + pltpu.roll(pe, 32, 1) * sb_ref[...])


def _q_proj_body(qa_ref, g_ref, w_ref, c_ref, sa_ref, sb_ref, o_ref, h_scr, *, scale):
    _norm_into(qa_ref, g_ref, h_scr)
    for h in range(MLA_HEADS):
        acc = jnp.dot(h_scr[...], w_ref[:, h * MLA_QK:(h + 1) * MLA_QK], preferred_element_type=F32)
        o_ref[:, h * MLA_QK:h * MLA_QK + LANE] = (acc[:, :LANE] * scale).astype(o_ref.dtype)
        pe = _rope128(acc[:, LANE:], c_ref, sa_ref, sb_ref)
        o_ref[:, h * MLA_QK + LANE:(h + 1) * MLA_QK] = pe.astype(o_ref.dtype)


def q_proj(z, gain, w, tabs, *, seq, tm, scale):
    t = z.shape[0]
    r = w.shape[0]
    n = w.shape[1]
    per_seq = seq // tm
    tab_spec = pl.BlockSpec((tm, LANE), lambda i: (i % per_seq, 0))
    return pl.pallas_call(
        functools.partial(_q_proj_body, scale=scale),
        grid=(t // tm,),
        in_specs=[
            pl.BlockSpec((tm, r), lambda i: (i, 0)),
            pl.BlockSpec((1, r), lambda i: (0, 0)),
            pl.BlockSpec((r, n), lambda i: (0, 0)),
            tab_spec, tab_spec, tab_spec,
        ],
        out_specs=pl.BlockSpec((tm, n), lambda i: (i, 0)),
        out_shape=jax.ShapeDtypeStruct((t, n), BF16),
        scratch_shapes=[pltpu.VMEM((tm, r), BF16)],
        compiler_params=_cparams("parallel"),
        name="mla_q_proj",
    )(z, gain.reshape(1, r), w, *tabs)


def _kv_proj_body(ckv_ref, kpe_ref, g_ref, w_ref, c_ref, sa_ref, sb_ref, k_ref, v_ref, h_scr):
    _norm_into(ckv_ref, g_ref, h_scr)
    pe = _rope128(kpe_ref[...].astype(F32), c_ref, sa_ref, sb_ref).astype(k_ref.dtype)
    nk = MLA_HEADS * MLA_NOPE
    for h in range(MLA_HEADS):
        kn = jnp.dot(h_scr[...], w_ref[:, h * MLA_NOPE:(h + 1) * MLA_NOPE], preferred_element_type=F32)
        k_ref[:, h * MLA_QK:h * MLA_QK + LANE] = kn.astype(k_ref.dtype)
        k_ref[:, h * MLA_QK + LANE:(h + 1) * MLA_QK] = pe
    v = jnp.dot(h_scr[...], w_ref[:, nk:], preferred_element_type=F32)
    v_ref[...] = v.astype(v_ref.dtype)


def kv_proj(z, gain, w, tabs, *, seq, tm):
    t = z.shape[0]
    r = w.shape[0]
    per_seq = seq // tm
    nv = MLA_HEADS * MLA_V
    tab_spec = pl.BlockSpec((tm, LANE), lambda i: (i % per_seq, 0))
    return pl.pallas_call(
        _kv_proj_body,
        grid=(t // tm,),
        in_specs=[
            pl.BlockSpec((tm, r), lambda i: (i, MLA_Q_RANK // r)),
            pl.BlockSpec((tm, LANE), lambda i: (i, (MLA_Q_RANK + r) // LANE)),
            pl.BlockSpec((1, r), lambda i: (0, 0)),
            pl.BlockSpec(w.shape, lambda i: (0, 0)),
            tab_spec, tab_spec, tab_spec,
        ],
        out_specs=[
            pl.BlockSpec((tm, MLA_HEADS * MLA_QK), lambda i: (i, 0)),
            pl.BlockSpec((tm, nv), lambda i: (i, 0)),
        ],
        out_shape=[
            jax.ShapeDtypeStruct((t, MLA_HEADS * MLA_QK), BF16),
            jax.ShapeDtypeStruct((t, nv), BF16),
        ],
        scratch_shapes=[pltpu.VMEM((tm, r), BF16)],
        compiler_params=_cparams("parallel"),
        name="mla_kv_proj",
    )(z, z, gain.reshape(1, r), w, *tabs)


def _dot_nt(a, b):
    return lax.dot_general(a, b, (((1,), (1,)), ((), ())), preferred_element_type=F32)


def _mla_attn_body(q_ref, k_ref, v_ref, o_ref, *, tq):
    qi = pl.program_id(2)
    q = q_ref[...]

    def step(j, carry, diagonal):
        m, l, acc = carry
        start = pl.multiple_of(j * tq, tq)
        s = _dot_nt(q, k_ref[pl.ds(start, tq), :])
        if diagonal:
            row = lax.broadcasted_iota(jnp.int32, s.shape, 0)
            col = lax.broadcasted_iota(jnp.int32, s.shape, 1)
            s = jnp.where(col <= row, s, NEG)
        m_new = jnp.maximum(m, jnp.max(s, axis=-1, keepdims=True))
        alpha = jnp.exp(m - m_new)
        p = jnp.exp(s - m_new)
        l = alpha * l + jnp.sum(p, axis=-1, keepdims=True)
        pv = jnp.dot(p.astype(v_ref.dtype), v_ref[pl.ds(start, tq), :], preferred_element_type=F32)
        return m_new, l, alpha * acc + pv

    init = (jnp.full((tq, 1), NEG, F32), jnp.zeros((tq, 1), F32), jnp.zeros((tq, v_ref.shape[1]), F32))
    carry = lax.fori_loop(0, qi, lambda j, c: step(j, c, False), init)
    _, l, acc = step(qi, carry, True)
    o_ref[...] = (acc / l).astype(o_ref.dtype)


def mla_attention(q, k, v, *, batch, seq, tq):
    t = batch * seq
    q3 = q.reshape(batch, seq, MLA_HEADS * MLA_QK)
    k3 = k.reshape(batch, seq, MLA_HEADS * MLA_QK)
    v3 = v.reshape(batch, seq, MLA_HEADS * MLA_V)
    out = pl.pallas_call(
        functools.partial(_mla_attn_body, tq=tq),
        grid=(batch, MLA_HEADS, seq // tq),
        in_specs=[
            pl.BlockSpec((None, tq, MLA_QK), lambda b, h, i: (b, i, h)),
            pl.BlockSpec((None, seq, MLA_QK), lambda b, h, i: (b, 0, h)),
            pl.BlockSpec((None, seq, MLA_V), lambda b, h, i: (b, 0, h)),
        ],
        out_specs=pl.BlockSpec((None, tq, MLA_V), lambda b, h, i: (b, i, h)),
        out_shape=jax.ShapeDtypeStruct((batch, seq, MLA_HEADS * MLA_V), BF16),
        compiler_params=_cparams("parallel", "parallel", "arbitrary"),
        name="mla_attention",
    )(q3, k3, v3)
    return out.reshape(t, MLA_HEADS * MLA_V)


def _dil_attn_body(q_ref, kp_ref, kc_ref, vp_ref, vc_ref, b_ref, o_ref, lse_ref, *, scale):
    blk = q_ref.shape[0]
    lane = lax.broadcasted_iota(jnp.int32, (blk, LANE), 1)
    lse_tile = jnp.zeros((blk, LANE), F32)
    for h in range(DIL_HPG):
        sl = slice(h * DIL_DH, (h + 1) * DIL_DH)
        q = q_ref[:, sl]
        bias = b_ref[h]
        sp = _dot_nt(q, kp_ref[:, sl]) * scale + bias[:, :blk]
        sc = _dot_nt(q, kc_ref[:, sl]) * scale + bias[:, blk:]
        m = jnp.maximum(jnp.max(sp, axis=-1, keepdims=True), jnp.max(sc, axis=-1, keepdims=True))
        pp = jnp.exp(sp - m)
        pc = jnp.exp(sc - m)
        l = jnp.sum(pp, axis=-1, keepdims=True) + jnp.sum(pc, axis=-1, keepdims=True)
        o = (jnp.dot(pp.astype(vp_ref.dtype), vp_ref[:, sl], preferred_element_type=F32)
             + jnp.dot(pc.astype(vc_ref.dtype), vc_ref[:, sl], preferred_element_type=F32))
        o_ref[:, sl] = (o / l).astype(o_ref.dtype)
        lse_tile = jnp.where(lane == h, m + jnp.log(l), lse_tile)
    lse_ref[...] = lse_tile


def dilated_group_attention(dil, bias, *, group, dilation, batch, seq):
    width = dil.shape[1]
    hw = DIL_HPG * DIL_DH
    per_class = width // hw
    ln = seq // dilation
    nb = ln // DIL_BLOCK
    dv = dil.reshape(batch, ln, dilation * width)
    q_col = lambda r: per_class * r + 3 * group
    o, lse = pl.pallas_call(
        functools.partial(_dil_attn_body, scale=DIL_DH ** -0.5),
        grid=(batch, dilation, nb),
        in_specs=[
            pl.BlockSpec((None, DIL_BLOCK, hw), lambda b, r, n: (b, n, q_col(r))),
            pl.BlockSpec((None, DIL_BLOCK, hw), lambda b, r, n: (b, jnp.maximum(n - 1, 0), q_col(r) + 1)),
            pl.BlockSpec((None, DIL_BLOCK, hw), lambda b, r, n: (b, n, q_col(r) + 1)),
            pl.BlockSpec((None, DIL_BLOCK, hw), lambda b, r, n: (b, jnp.maximum(n - 1, 0), q_col(r) + 2)),
            pl.BlockSpec((None, DIL_BLOCK, hw), lambda b, r, n: (b, n, q_col(r) + 2)),
            pl.BlockSpec((None, DIL_HPG, DIL_BLOCK, 2 * DIL_BLOCK), lambda b, r, n: (jnp.minimum(n, 1), 0, 0, 0)),
        ],
        out_specs=[
            pl.BlockSpec((None, DIL_BLOCK, hw), lambda b, r, n: (b, n, r)),
            pl.BlockSpec((None, DIL_BLOCK, LANE), lambda b, r, n: (b, n, r)),
        ],
        out_shape=[
            jax.ShapeDtypeStruct((batch, ln, dilation * hw), BF16),
            jax.ShapeDtypeStruct((batch, ln, dilation * LANE), F32),
        ],
        compiler_params=_cparams("parallel", "parallel", "arbitrary"),
        name=f"dilated_attention_g{group}",
    )(dv, dv, dv, dv, dv, bias)
    t = batch * seq
    return o.reshape(t, hw), lse.reshape(t, LANE)


def _dil_combine_body(o0_ref, o1_ref, o2_ref, l0_ref, l1_ref, l2_ref, out_ref):
    o_refs = (o0_ref, o1_ref, o2_ref)
    for h in range(DIL_HPG):
        sl = slice(h * DIL_DH, (h + 1) * DIL_DH)
        ls = [r[:, h:h + 1] for r in (l0_ref, l1_ref, l2_ref)]
        m = jnp.maximum(jnp.maximum(ls[0], ls[1]), ls[2])
        es = [jnp.exp(x - m) for x in ls]
        den = es[0] + es[1] + es[2]
        acc = sum((e / den) * r[:, sl].astype(F32) for e, r in zip(es, o_refs))
        out_ref[:, sl] = acc.astype(out_ref.dtype)


def dilated_combine(os_, lses, *, tm):
    t, hw = os_[0].shape
    o_spec = pl.BlockSpec((tm, hw), lambda i: (i, 0))
    l_spec = pl.BlockSpec((tm, LANE), lambda i: (i, 0))
    return pl.pallas_call(
        _dil_combine_body,
        grid=(t // tm,),
        in_specs=[o_spec] * 3 + [l_spec] * 3,
        out_specs=o_spec,
        out_shape=jax.ShapeDtypeStruct((t, hw), BF16),
        compiler_params=_cparams("parallel"),
        name="dilated_combine",
    )(*os_, *lses)


def _t5_bucket_np(dist):
    max_exact = REL_BUCKETS // 2
    large = max_exact + (np.log(np.maximum(dist, 1).astype(np.float32) / max_exact)
                         / math.log(REL_MAX_DIST / max_exact) * (REL_BUCKETS - max_exact)).astype(np.int32)
    large = np.minimum(large, REL_BUCKETS - 1)
    return np.where(dist < max_exact, dist, large)


def dilated_bias_tables(rel_bias):
    iq = np.arange(DIL_BLOCK)[:, None]
    jk = np.arange(2 * DIL_BLOCK)[None, :]
    dist_c = DIL_BLOCK + iq - jk
    tables = []
    for g, (window, dilation) in enumerate(DIL_PAIRS):
        span = window // dilation
        band = (dist_c >= 0) & (dist_c <= span)
        bucket = _t5_bucket_np(np.clip(dist_c, 0, None) * dilation)
        bias = rel_bias[bucket][:, :, g * DIL_HPG:(g + 1) * DIL_HPG].astype(F32).transpose(2, 0, 1)
        general = jnp.where(band[None], bias, NEG)
        first = jnp.where((band & (jk >= DIL_BLOCK))[None], bias, NEG)
        tables.append(jnp.stack([first, general]))
    return tables


def _merge_body(a_ref, b_ref, wa_ref, wb_ref, ga_ref, gb_ref, o_ref):
    ya = jnp.dot(a_ref[...], wa_ref[...], preferred_element_type=F32)
    yb = jnp.dot(b_ref[...], wb_ref[...], preferred_element_type=F32)
    o_ref[...] = (ga_ref[...].astype(F32) * ya + gb_ref[...].astype(F32) * yb).astype(o_ref.dtype)


def merge_branches(o_a, o_b, w_a, w_b, z, *, gate_col, tm, tn):
    t, ka = o_a.shape
    kb = o_b.shape[1]
    n = w_a.shape[1]
    g0 = gate_col // tn
    return pl.pallas_call(
        _merge_body,
        grid=(t // tm, n // tn),
        in_specs=[
            pl.BlockSpec((tm, ka), lambda i, j: (i, 0)),
            pl.BlockSpec((tm, kb), lambda i, j: (i, 0)),
            pl.BlockSpec((ka, tn), lambda i, j: (0, j)),
            pl.BlockSpec((kb, tn), lambda i, j: (0, j)),
            pl.BlockSpec((tm, tn), lambda i, j: (i, g0 + j)),
            pl.BlockSpec((tm, tn), lambda i, j: (i, g0 + n // tn + j)),
        ],
        out_specs=pl.BlockSpec((tm, tn), lambda i, j: (i, j)),
        out_shape=jax.ShapeDtypeStruct((t, n), BF16),
        compiler_params=_cparams("parallel", "arbitrary"),
        name="merge_branches",
    )(o_a, o_b, w_a, w_b, z, z)


def _mm_residual_body(a_ref, w_ref, x_ref, o_ref):
    o_ref[...] = x_ref[...] + jnp.dot(a_ref[...], w_ref[...], preferred_element_type=F32)


def matmul_residual(a, w, x, *, tm, tn):
    t, k = a.shape
    n = w.shape[1]
    return pl.pallas_call(
        _mm_residual_body,
        grid=(t // tm, n // tn),
        in_specs=[
            pl.BlockSpec((tm, k), lambda i, j: (i, 0)),
            pl.BlockSpec((k, tn), lambda i, j: (0, j)),
            pl.BlockSpec((tm, tn), lambda i, j: (i, j)),
        ],
        out_specs=pl.BlockSpec((tm, tn), lambda i, j: (i, j)),
        out_shape=jax.ShapeDtypeStruct((t, n), F32),
        compiler_params=_cparams("parallel", "arbitrary"),
        name="matmul_residual",
    )(a, w, x)


def _swiglu_up(h, wg, wu):
    g = jnp.dot(h, wg, preferred_element_type=F32)
    u = jnp.dot(h, wu, preferred_element_type=F32)
    return g * _sigmoid(g) * u


def _swiglu_up_body(x_ref, g_ref, wg_ref, wu_ref, o_ref, h_scr):
    @pl.when(pl.program_id(1) == 0)
    def _():
        _norm_into(x_ref, g_ref, h_scr)

    o_ref[...] = _swiglu_up(h_scr[...], wg_ref[...], wu_ref[...]).astype(o_ref.dtype)


def swiglu_up(x, gain, w_gate, w_up, *, tm, tn):
    t, k = x.shape
    n = w_gate.shape[1]
    w_spec = pl.BlockSpec((k, tn), lambda i, j: (0, j))
    return pl.pallas_call(
        _swiglu_up_body,
        grid=(t // tm, n // tn),
        in_specs=[
            pl.BlockSpec((tm, k), lambda i, j: (i, 0)),
            pl.BlockSpec((1, k), lambda i, j: (0, 0)),
            w_spec, w_spec,
        ],
        out_specs=pl.BlockSpec((tm, tn), lambda i, j: (i, j)),
        out_shape=jax.ShapeDtypeStruct((t, n), BF16),
        scratch_shapes=[pltpu.VMEM((tm, k), BF16)],
        compiler_params=_cparams("parallel", "arbitrary"),
        name="swiglu_up",
    )(x, gain.reshape(1, k), w_gate, w_up)


def _grouped_up_body(te_ref, nu_ref, x_ref, g_ref, wg_ref, wu_ref, o_ref, h_scr):
    i = pl.program_id(0)

    @pl.when(i < nu_ref[0])
    def _():
        @pl.when(pl.program_id(1) == 0)
        def _():
            _norm_into(x_ref, g_ref, h_scr)

        o_ref[...] = _swiglu_up(h_scr[...], wg_ref[...], wu_ref[...]).astype(o_ref.dtype)

    @pl.when(i >= nu_ref[0])
    def _():
        o_ref[...] = jnp.zeros_like(o_ref)


def grouped_swiglu_up(xs, gain, w_gate, w_up, tile_expert, n_used, *, tm, tn):
    r, k = xs.shape
    n = w_gate.shape[2]
    nj = n // tn

    def row(i, nu):
        return jnp.minimum(i, nu[0] - 1)

    def col(i, j, nu):
        return jnp.where(i < nu[0], j, nj - 1)

    w_spec = pl.BlockSpec((None, k, tn), lambda i, j, te, nu: (te[i], 0, col(i, j, nu)))
    return pl.pallas_call(
        _grouped_up_body,
        grid_spec=pltpu.PrefetchScalarGridSpec(
            num_scalar_prefetch=2,
            grid=(r // tm, nj),
            in_specs=[
                pl.BlockSpec((tm, k), lambda i, j, te, nu: (row(i, nu), 0)),
                pl.BlockSpec((1, k), lambda i, j, te, nu: (0, 0)),
                w_spec, w_spec,
            ],
            out_specs=pl.BlockSpec((tm, tn), lambda i, j, te, nu: (i, j)),
            scratch_shapes=[pltpu.VMEM((tm, k), BF16)],
        ),
        out_shape=jax.ShapeDtypeStruct((r, n), BF16),
        compiler_params=_cparams("arbitrary", "arbitrary"),
        name="grouped_swiglu_up",
    )(tile_expert, n_used, xs, gain.reshape(1, k), w_gate, w_up)


def _grouped_down_body(te_ref, nu_ref, a_ref, w_ref, o_ref):
    i = pl.program_id(0)

    @pl.when(i < nu_ref[0])
    def _():
        o_ref[...] = jnp.dot(a_ref[...], w_ref[...], preferred_element_type=F32)

    @pl.when(i >= nu_ref[0])
    def _():
        o_ref[...] = jnp.zeros_like(o_ref)


def grouped_down(a, w_down, tile_expert, n_used, *, tm, tn):
    r, k = a.shape
    n = w_down.shape[2]
    nj = n // tn

    def row(i, nu):
        return jnp.minimum(i, nu[0] - 1)

    def col(i, j, nu):
        return jnp.where(i < nu[0], j, nj - 1)

    return pl.pallas_call(
        _grouped_down_body,
        grid_spec=pltpu.PrefetchScalarGridSpec(
            num_scalar_prefetch=2,
            grid=(r // tm, nj),
            in_specs=[
                pl.BlockSpec((tm, k), lambda i, j, te, nu: (row(i, nu), 0)),
                pl.BlockSpec((None, k, tn), lambda i, j, te, nu: (te[i], 0, col(i, j, nu))),
            ],
            out_specs=pl.BlockSpec((tm, tn), lambda i, j, te, nu: (i, j)),
        ),
        out_shape=jax.ShapeDtypeStruct((r, n), F32),
        compiler_params=_cparams("arbitrary", "arbitrary"),
        name="grouped_down",
    )(tile_expert, n_used, a, w_down)


def _router_body(x_ref, g_ref, wr_ref, idx_ref, wt_ref):
    h = _rms_rows(x_ref[...], g_ref[...])
    logits = lax.dot_general(wr_ref[...], h, (((1,), (1,)), ((), ())),
                             precision=lax.Precision.HIGHEST, preferred_element_type=F32)
    e = lax.broadcasted_iota(jnp.int32, logits.shape, 0)
    n_e = logits.shape[0]
    m1 = jnp.max(logits, axis=0, keepdims=True)
    i1 = jnp.min(jnp.where(logits == m1, e, n_e), axis=0, keepdims=True)
    rest = jnp.where(e == i1, -jnp.inf, logits)
    m2 = jnp.max(rest, axis=0, keepdims=True)
    i2 = jnp.min(jnp.where(rest == m2, e, n_e), axis=0, keepdims=True)
    t = jnp.exp(m2 - m1)
    idx_ref[0:1, :] = i1
    idx_ref[1:2, :] = i2
    wt_ref[0:1, :] = 1.0 / (1.0 + t)
    wt_ref[1:2, :] = t / (1.0 + t)


def moe_router(x, gain, w_router_t, *, tm):
    t, k = x.shape
    n_e = w_router_t.shape[0]
    out_spec = pl.BlockSpec((TOP_K, tm), lambda i: (0, i))
    return pl.pallas_call(
        _router_body,
        grid=(t // tm,),
        in_specs=[
            pl.BlockSpec((tm, k), lambda i: (i, 0)),
            pl.BlockSpec((1, k), lambda i: (0, 0)),
            pl.BlockSpec((n_e, k), lambda i: (0, 0)),
        ],
        out_specs=[out_spec, out_spec],
        out_shape=[jax.ShapeDtypeStruct((TOP_K, t), jnp.int32), jax.ShapeDtypeStruct((TOP_K, t), F32)],
        compiler_params=_cparams("parallel"),
        name="moe_router",
    )(x, gain.reshape(1, k), w_router_t)


def _row_copy(src_ref, o_ref, sem, src_row, dst_row):
    return pltpu.make_async_copy(src_ref.at[pl.ds(src_row, 1), :], o_ref.at[pl.ds(dst_row, 1), :], sem)


def _gather_body(idx_ref, src_ref, o_ref, sem):
    rows = o_ref.shape[0]

    def issue(r, c):
        _row_copy(src_ref, o_ref, sem, idx_ref[0, 0, r], r).start()
        return c

    lax.fori_loop(0, rows, issue, 0)

    def drain(r, c):
        _row_copy(src_ref, o_ref, sem, 0, r).wait()
        return c

    lax.fori_loop(0, rows, drain, 0)


def gather_rows(src, idx, *, rows):
    n = idx.shape[0]
    d = src.shape[1]
    steps = n // rows
    return pl.pallas_call(
        _gather_body,
        grid=(steps,),
        in_specs=[
            pl.BlockSpec((1, 1, rows), lambda i: (i, 0, 0), memory_space=pltpu.SMEM),
            pl.BlockSpec(memory_space=pl.ANY),
        ],
        out_specs=pl.BlockSpec((rows, d), lambda i: (i, 0)),
        out_shape=jax.ShapeDtypeStruct((n, d), src.dtype),
        scratch_shapes=[pltpu.SemaphoreType.DMA(())],
        compiler_params=_cparams("arbitrary"),
        name="gather_rows",
    )(idx.reshape(steps, 1, rows), src)


def _moe_combine_body(x_ref, y0_ref, y1_ref, w_ref, o_ref):
    w = w_ref[...]
    o_ref[...] = x_ref[...] + w[:, 0:1] * y0_ref[...] + w[:, 1:2] * y1_ref[...]


def moe_combine(x, yg, wts, *, tm):
    t, d = x.shape
    off = t // tm
    return pl.pallas_call(
        _moe_combine_body,
        grid=(t // tm,),
        in_specs=[
            pl.BlockSpec((tm, d), lambda i: (i, 0)),
            pl.BlockSpec((tm, d), lambda i: (i, 0)),
            pl.BlockSpec((tm, d), lambda i: (i + off, 0)),
            pl.BlockSpec((tm, TOP_K), lambda i: (i, 0)),
        ],
        out_specs=pl.BlockSpec((tm, d), lambda i: (i, 0)),
        out_shape=jax.ShapeDtypeStruct((t, d), F32),
        compiler_params=_cparams("parallel"),
        name="moe_combine",
    )(x, yg, yg, wts)


def moe_dispatch_plan(top_idx, *, tm):
    k, t = top_idx.shape
    n_rows = k * t + N_EXPERTS * tm
    n_tiles = n_rows // tm
    e = top_idx.reshape(-1)
    onehot = (e[:, None] == jnp.arange(N_EXPERTS, dtype=jnp.int32)[None, :]).astype(jnp.int32)
    csum = jnp.cumsum(onehot, axis=0)
    counts = csum[-1]
    rank = jnp.sum((csum - onehot) * onehot, axis=1)
    padded = ((counts + tm - 1) // tm) * tm
    ends = jnp.cumsum(padded)
    starts = ends - padded
    pos = starts[e] + rank
    token = jnp.tile(jnp.arange(t, dtype=jnp.int32), k)
    row_token = jnp.zeros((n_rows,), jnp.int32).at[pos].set(token)
    n_used = (ends[-1] // tm).astype(jnp.int32)
    tile_start = jnp.minimum(jnp.arange(n_tiles, dtype=jnp.int32), n_used - 1) * tm
    tile_expert = jnp.sum((tile_start[:, None] >= ends[None, :]).astype(jnp.int32), axis=1)
    return row_token, pos.astype(jnp.int32), tile_expert.astype(jnp.int32), n_used.reshape(1)


def moe_ffn(x, gain, w_router, w_gate, w_up, w_down):
    idx, wts = moe_router(x, gain, w_router.T, tm=TM_MOE)
    row_token, pos, tile_expert, n_used = moe_dispatch_plan(idx, tm=TM_MOE)
    xs = gather_rows(x, row_token, rows=GATHER_ROWS)
    a = grouped_swiglu_up(xs, gain, w_gate, w_up, tile_expert, n_used, tm=TM_MOE, tn=TN)
    y = grouped_down(a, w_down, tile_expert, n_used, tm=TM_MOE, tn=TN)
    yg = gather_rows(y, pos, rows=GATHER_ROWS)
    return moe_combine(x, yg, wts.T, tm=TM_MOE)


def _rmsnorm_body(x_ref, g_ref, o_ref):
    o_ref[...] = _rms_rows(x_ref[...], g_ref[...])


def rmsnorm(x, gain, *, tm):
    t, k = x.shape
    return pl.pallas_call(
        _rmsnorm_body,
        grid=(t // tm,),
        in_specs=[pl.BlockSpec((tm, k), lambda i: (i, 0)), pl.BlockSpec((1, k), lambda i: (0, 0))],
        out_specs=pl.BlockSpec((tm, k), lambda i: (i, 0)),
        out_shape=jax.ShapeDtypeStruct((t, k), F32),
        compiler_params=_cparams("parallel"),
        name="final_rmsnorm",
    )(x, gain.reshape(1, k))


def _rope_tables(seq, scale):
    half = MLA_ROPE // 2
    inv = ROPE_THETA ** (-jnp.arange(half, dtype=F32) / half)
    ang = jnp.arange(seq).astype(F32)[:, None] * inv[None, :]
    cos, sin = jnp.cos(ang) * scale, jnp.sin(ang) * scale
    z = jnp.zeros_like(cos)
    cat = lambda *p: jnp.concatenate(p, axis=1)
    return cat(cos, cos, z, z), cat(-sin, z, z, z), cat(z, sin, z, z)


def _layout_w_in(w, d_model):
    off_kva = MLA_Q_RANK
    off_dil = off_kva + MLA_KV_RANK + MLA_ROPE
    col_dil = 3 * DIL_HEADS * DIL_DH
    off_gate = off_dil + col_dil
    pad = jnp.zeros((w.shape[0], 2 * LANE - MLA_ROPE), w.dtype)
    small = jnp.concatenate([w[:, :off_dil], pad, w[:, off_gate:]], axis=1)
    cols = np.arange(col_dil).reshape(3, DIL_GROUPS, DIL_HPG * DIL_DH)
    perm = off_dil + cols.transpose(1, 0, 2).reshape(-1)
    return small.astype(BF16), w[:, perm].astype(BF16)


def _layout_w_uq(w):
    r = w.shape[0]
    w = w.reshape(r, MLA_HEADS, MLA_NOPE + MLA_ROPE)
    w = jnp.pad(w, ((0, 0), (0, 0), (0, MLA_QK - MLA_NOPE - MLA_ROPE)))
    return w.reshape(r, MLA_HEADS * MLA_QK).astype(BF16)


def _layout_w_ukv(w):
    r = w.shape[0]
    w = w.reshape(r, MLA_HEADS, MLA_NOPE + MLA_V)
    k = w[:, :, :MLA_NOPE].reshape(r, MLA_HEADS * MLA_NOPE)
    v = w[:, :, MLA_NOPE:].reshape(r, MLA_HEADS * MLA_V)
    return jnp.concatenate([k, v], axis=1).astype(BF16)


def kernel(x, mix_norm, w_in, q_norm, w_uq, kv_norm, w_ukv, w_o_mla, w_o_dil, w_out, rel_bias,
           ffn_norm, w_ffn_gate, w_ffn_up, w_ffn_down, w_router, w_exp_gate, w_exp_up, w_exp_down,
           final_norm):
    batch, seq, d_model = x.shape
    depth = w_in.shape[0]
    t = batch * seq
    x = x.reshape(t, d_model)

    q_scale = (MLA_NOPE + MLA_ROPE) ** -0.5
    q_tabs = _rope_tables(seq, q_scale)
    k_tabs = _rope_tables(seq, 1.0)
    bias_tabs = dilated_bias_tables(rel_bias)
    gate_col = MLA_Q_RANK + MLA_KV_RANK + 2 * LANE

    for layer in range(depth):
        w_small, w_dil = _layout_w_in(w_in[layer], d_model)
        z = norm_matmul(x, mix_norm[layer], w_small, tm=TM, tn=TN, sig_from=gate_col // TN)
        dil = norm_matmul(x, mix_norm[layer], w_dil, tm=TM, tn=TN)

        q = q_proj(z, q_norm[layer], _layout_w_uq(w_uq[layer]), q_tabs, seq=seq, tm=TQ, scale=q_scale)
        k, v = kv_proj(z, kv_norm[layer], _layout_w_ukv(w_ukv[layer]), k_tabs, seq=seq, tm=TQ)
        o_mla = mla_attention(q, k, v, batch=batch, seq=seq, tq=TQ)

        os_, lses = [], []
        for g, (_, dilation) in enumerate(DIL_PAIRS):
            o_g, lse_g = dilated_group_attention(dil, bias_tabs[g], group=g, dilation=dilation,
                                                 batch=batch, seq=seq)
            os_.append(o_g)
            lses.append(lse_g)
        o_dil = dilated_combine(os_, lses, tm=TM)

        merged = merge_branches(o_mla, o_dil, w_o_mla[layer].astype(BF16), w_o_dil[layer].astype(BF16),
                                z, gate_col=gate_col, tm=TM, tn=TN)
        x = matmul_residual(merged, w_out[layer].astype(BF16), x, tm=TM, tn=TN)

        i = layer // 2
        if layer % 2 == 0:
            a = swiglu_up(x, ffn_norm[layer], w_ffn_gate[i].astype(BF16), w_ffn_up[i].astype(BF16),
                          tm=TM, tn=TN)
            x = matmul_residual(a, w_ffn_down[i].astype(BF16), x, tm=TM_DOWN, tn=TN)
        else:
            x = moe_ffn(x, ffn_norm[layer], w_router[i], w_exp_gate[i].astype(BF16),
                        w_exp_up[i].astype(BF16), w_exp_down[i].astype(BF16))

    return rmsnorm(x, final_norm, tm=TM).reshape(batch, seq, d_model)
```

```python
import functools
import math

import numpy as np
import jax
import jax.numpy as jnp
from jax import lax
from jax.experimental import pallas as pl
from jax.experimental.pallas import tpu as pltpu

F32 = jnp.float32
BF16 = jnp.bfloat16

EPS = 1e-6
NEG = -1e30

MLA_HEADS = 16
MLA_Q_RANK = 512
MLA_KV_RANK = 256
MLA_NOPE = 128
MLA_ROPE = 64
MLA_V = 128
ROPE_THETA = 10000.0
DIL_PAIRS = ((128, 1), (512, 4), (2048, 16))
DIL_GROUPS = 3
DIL_HPG = 8
DIL_HEADS = DIL_GROUPS * DIL_HPG
DIL_DH = 128
DIL_BLOCK = 128
REL_BUCKETS = 32
REL_MAX_DIST = 2048
N_EXPERTS = 8
TOP_K = 2

LANE = 128
MLA_QK = 256
VMEM_LIMIT = 56 * 2**20

TM = 1024
TM_DOWN = 512
TN = 512
TM_MOE = 512
TQ = 512
MLA_ROW_RUN = 256
DIL_ROW_CHUNK = 64
TM_COMBINE = 512
GATHER_ROWS = 256
NORM_CHUNK = 256


def _cparams(*sem):
    return pltpu.CompilerParams(dimension_semantics=sem, vmem_limit_bytes=VMEM_LIMIT)


def _rms_rows(x, g):
    ms = jnp.mean(x * x, axis=-1, keepdims=True)
    return x * lax.rsqrt(ms + EPS) * g


def _norm_into(x_ref, g_ref, h_scr):
    rows = x_ref.shape[0]
    chunk = min(NORM_CHUNK, rows)
    for r in range(0, rows, chunk):
        x = x_ref[r:r + chunk, :].astype(F32)
        h_scr[r:r + chunk, :] = _rms_rows(x, g_ref[...]).astype(h_scr.dtype)


def _sigmoid(x):
    return 1.0 / (1.0 + jnp.exp(-x))


def _norm_mm_body(x_ref, g_ref, w_ref, o_ref, h_scr, *, sig_from):
    j = pl.program_id(1)

    @pl.when(j == 0)
    def _():
        _norm_into(x_ref, g_ref, h_scr)

    acc = jnp.dot(h_scr[...], w_ref[...], preferred_element_type=F32)
    if sig_from is None:
        o_ref[...] = acc.astype(o_ref.dtype)
    else:
        @pl.when(j < sig_from)
        def _():
            o_ref[...] = acc.astype(o_ref.dtype)

        @pl.when(j >= sig_from)
        def _():
            o_ref[...] = _sigmoid(acc).astype(o_ref.dtype)


def norm_matmul(x, gain, w, *, tm, tn, sig_from=None):
    t, k = x.shape
    n = w.shape[1]
    return pl.pallas_call(
        functools.partial(_norm_mm_body, sig_from=sig_from),
        grid=(t // tm, n // tn),
        in_specs=[
            pl.BlockSpec((tm, k), lambda i, j: (i, 0)),
            pl.BlockSpec((1, k), lambda i, j: (0, 0)),
            pl.BlockSpec((k, tn), lambda i, j: (0, j)),
        ],
        out_specs=pl.BlockSpec((tm, tn), lambda i, j: (i, j)),
        out_shape=jax.ShapeDtypeStruct((t, n), BF16),
        scratch_shapes=[pltpu.VMEM((tm, k), BF16)],
        compiler_params=_cparams("parallel", "arbitrary"),
        name="norm_matmul",
    )(x, gain.reshape(1, k), w)


def _dil_proj_body(x_ref, g_ref, w_ref, o0_ref, o1_ref, o2_ref, h_scr, hf_scr, *, tiles_per_group):
    j = pl.program_id(1)
    tm, k = x_ref.shape
    o_refs = (o0_ref, o1_ref, o2_ref)

    @pl.when(j == 0)
    def _():
        chunk = min(NORM_CHUNK, tm)
        for r0 in range(0, tm, chunk):
            h = _rms_rows(x_ref[r0:r0 + chunk, :], g_ref[...])
            for c in range(k // LANE):
                hf_scr[c, r0:r0 + chunk, :] = h[:, c * LANE:(c + 1) * LANE]
        for g, o_ref in enumerate(o_refs):
            d = o_ref.shape[0]
            cs = tm // d
            for r in range(d):
                for c in range(k // LANE):
                    rows = hf_scr[c, pl.ds(r, cs, stride=d), :] if d > 1 else hf_scr[c]
                    h_scr[g, r * cs:(r + 1) * cs, c * LANE:(c + 1) * LANE] = rows.astype(h_scr.dtype)

    group = j // tiles_per_group
    acc = jnp.dot(h_scr[group], w_ref[...], preferred_element_type=F32)
    for g, o_ref in enumerate(o_refs):
        d = o_ref.shape[0]
        cs = tm // d

        @pl.when(group == g)
        def _(o_ref=o_ref, d=d, cs=cs):
            for r in range(d):
                o_ref[r] = acc[r * cs:(r + 1) * cs, :].astype(o_ref.dtype)


def dilated_qkv_proj(x, gain, w, *, batch, seq, tm, tn):
    t, k = x.shape
    unit = DIL_HPG * DIL_DH
    upt = unit // tn
    tpg = 3 * upt
    per_seq = seq // tm

    def w_col(j):
        return ((j % tpg) // upt) * (DIL_GROUPS * upt) + (j // tpg) * upt + j % upt

    out_specs, out_shape = [], []
    for g, (_, d) in enumerate(DIL_PAIRS):
        out_specs.append(pl.BlockSpec(
            (None, d, tm // d, tn),
            lambda i, j, g=g: (i // per_seq, 0, i % per_seq, jnp.clip(j - g * tpg, 0, tpg - 1))))
        out_shape.append(jax.ShapeDtypeStruct((batch, d, seq // d, 3 * unit), BF16))
    return pl.pallas_call(
        functools.partial(_dil_proj_body, tiles_per_group=tpg),
        grid=(t // tm, DIL_GROUPS * tpg),
        in_specs=[
            pl.BlockSpec((tm, k), lambda i, j: (i, 0)),
            pl.BlockSpec((1, k), lambda i, j: (0, 0)),
            pl.BlockSpec((k, tn), lambda i, j: (0, w_col(j))),
        ],
        out_specs=out_specs,
        out_shape=out_shape,
        scratch_shapes=[pltpu.VMEM((DIL_GROUPS, tm, k), BF16), pltpu.VMEM((k // LANE, tm, LANE), F32)],
        compiler_params=_cparams("parallel", "arbitrary"),
        name="dilated_qkv_proj",
    )(x, gain.reshape(1, k), w)


def _rope128(pe, c_ref, sa_ref, sb_ref):
    return (pe * c_ref[...]
            + pltpu.roll(pe, 96, 1) * sa_ref[...]
            + pltpu.roll(pe, 32, 1) * sb_ref[...])


def _q_proj_body(qa_ref, g_ref, w_ref, c_ref, sa_ref, sb_ref, o_ref, h_scr, *, scale):
    _norm_into(qa_ref, g_ref, h_scr)
    for h in range(MLA_HEADS):
        acc = jnp.dot(h_scr[...], w_ref[:, h * MLA_QK:(h + 1) * MLA_QK], preferred_element_type=F32)
        o_ref[:, h * MLA_QK:h * MLA_QK + LANE] = (acc[:, :LANE] * scale).astype(o_ref.dtype)
        pe = _rope128(acc[:, LANE:], c_ref, sa_ref, sb_ref)
        o_ref[:, h * MLA_QK + LANE:(h + 1) * MLA_QK] = pe.astype(o_ref.dtype)


def q_proj(z, gain, w, tabs, *, seq, tm, scale):
    t = z.shape[0]
    r = w.shape[0]
    n = w.shape[1]
    per_seq = seq // tm
    tab_spec = pl.BlockSpec((tm, LANE), lambda i: (i % per_seq, 0))
    return pl.pallas_call(
        functools.partial(_q_proj_body, scale=scale),
        grid=(t // tm,),
        in_specs=[
            pl.BlockSpec((tm, r), lambda i: (i, 0)),
            pl.BlockSpec((1, r), lambda i: (0, 0)),
            pl.BlockSpec((r, n), lambda i: (0, 0)),
            tab_spec, tab_spec, tab_spec,
        ],
        out_specs=pl.BlockSpec((tm, n), lambda i: (i, 0)),
        out_shape=jax.ShapeDtypeStruct((t, n), BF16),
        scratch_shapes=[pltpu.VMEM((tm, r), BF16)],
        compiler_params=_cparams("parallel"),
        name="mla_q_proj",
    )(z, gain.reshape(1, r), w, *tabs)


def _kv_proj_body(ckv_ref, kpe_ref, g_ref, w_ref, c_ref, sa_ref, sb_ref, k_ref, v_ref, h_scr):
    _norm_into(ckv_ref, g_ref, h_scr)
    pe = _rope128(kpe_ref[...].astype(F32), c_ref, sa_ref, sb_ref).astype(k_ref.dtype)
    nk = MLA_HEADS * MLA_NOPE
    for h in range(MLA_HEADS):
        kn = jnp.dot(h_scr[...], w_ref[:, h * MLA_NOPE:(h + 1) * MLA_NOPE], preferred_element_type=F32)
        k_ref[:, h * MLA_QK:h * MLA_QK + LANE] = kn.astype(k_ref.dtype)
        k_ref[:, h * MLA_QK + LANE:(h + 1) * MLA_QK] = pe
    v = jnp.dot(h_scr[...], w_ref[:, nk:], preferred_element_type=F32)
    v_ref[...] = v.astype(v_ref.dtype)


def kv_proj(z, gain, w, tabs, *, seq, tm):
    t = z.shape[0]
    r = w.shape[0]
    per_seq = seq // tm
    nv = MLA_HEADS * MLA_V
    tab_spec = pl.BlockSpec((tm, LANE), lambda i: (i % per_seq, 0))
    return pl.pallas_call(
        _kv_proj_body,
        grid=(t // tm,),
        in_specs=[
            pl.BlockSpec((tm, r), lambda i: (i, MLA_Q_RANK // r)),
            pl.BlockSpec((tm, LANE), lambda i: (i, (MLA_Q_RANK + r) // LANE)),
            pl.BlockSpec((1, r), lambda i: (0, 0)),
            pl.BlockSpec(w.shape, lambda i: (0, 0)),
            tab_spec, tab_spec, tab_spec,
        ],
        out_specs=[
            pl.BlockSpec((tm, MLA_HEADS * MLA_QK), lambda i: (i, 0)),
            pl.BlockSpec((tm, nv), lambda i: (i, 0)),
        ],
        out_shape=[
            jax.ShapeDtypeStruct((t, MLA_HEADS * MLA_QK), BF16),
            jax.ShapeDtypeStruct((t, nv), BF16),
        ],
        scratch_shapes=[pltpu.VMEM((tm, r), BF16)],
        compiler_params=_cparams("parallel"),
        name="mla_kv_proj",
    )(z, z, gain.reshape(1, r), w, *tabs)


def _dot_nt(a, b):
    return lax.dot_general(a, b, (((1,), (1,)), ((), ())), preferred_element_type=F32)


def _mla_attn_body(q_ref, k_ref, v_ref, o_ref, *, tq, tr):
    qi = pl.program_id(2)
    row = lax.broadcasted_iota(jnp.int32, (tr, tr), 0)
    col = lax.broadcasted_iota(jnp.int32, (tr, tr), 1)

    def run(first):
        rows = slice(first % tq, first % tq + tr)
        q = q_ref[rows, :]
        s_d = jnp.where(col <= row, _dot_nt(q, k_ref[first:first + tr, :]), NEG)
        m = jnp.max(s_d, axis=-1, keepdims=True)
        if first:
            s_o = _dot_nt(q, k_ref[:first, :])
            m = jnp.maximum(m, jnp.max(s_o, axis=-1, keepdims=True))
        p_d = jnp.exp2(s_d - m)
        l = jnp.sum(p_d, axis=-1, keepdims=True)
        o = jnp.dot(p_d.astype(v_ref.dtype), v_ref[first:first + tr, :], preferred_element_type=F32)
        if first:
            p_o = jnp.exp2(s_o - m)
            l = l + jnp.sum(p_o, axis=-1, keepdims=True)
            o = o + jnp.dot(p_o.astype(v_ref.dtype), v_ref[:first, :], preferred_element_type=F32)
        o_ref[rows, :] = (o / l).astype(o_ref.dtype)

    for n in range(k_ref.shape[0] // tq):
        @pl.when(qi == n)
        def _(n=n):
            for first in range(n * tq, (n + 1) * tq, tr):
                run(first)


def mla_attention(q, k, v, *, batch, seq, tq):
    t = batch * seq
    q3 = q.reshape(batch, seq, MLA_HEADS * MLA_QK)
    k3 = k.reshape(batch, seq, MLA_HEADS * MLA_QK)
    v3 = v.reshape(batch, seq, MLA_HEADS * MLA_V)
    out = pl.pallas_call(
        functools.partial(_mla_attn_body, tq=tq, tr=MLA_ROW_RUN),
        grid=(batch, MLA_HEADS, seq // tq),
        in_specs=[
            pl.BlockSpec((None, tq, MLA_QK), lambda b, h, i: (b, i, h)),
            pl.BlockSpec((None, seq, MLA_QK), lambda b, h, i: (b, 0, h)),
            pl.BlockSpec((None, seq, MLA_V), lambda b, h, i: (b, 0, h)),
        ],
        out_specs=pl.BlockSpec((None, tq, MLA_V), lambda b, h, i: (b, i, h)),
        out_shape=jax.ShapeDtypeStruct((batch, seq, MLA_HEADS * MLA_V), BF16),
        compiler_params=_cparams("parallel", "parallel", "arbitrary"),
        name="mla_attention",
    )(q3, k3, v3)
    return out.reshape(t, MLA_HEADS * MLA_V)


def _dil_attn_body(q_ref, kp_ref, kc_ref, vp_ref, vc_ref, b_ref, o_ref, lse_ref, s_scr, p_scr, m_scr,
                   *, scale, rc):
    blk = q_ref.shape[0]
    for h in range(DIL_HPG):
        sl = slice(h * DIL_DH, (h + 1) * DIL_DH)
        rows = slice(h * blk, (h + 1) * blk)
        q = q_ref[:, sl]
        s_scr[rows, :blk] = _dot_nt(q, kp_ref[:, sl]) * scale + b_ref[h, :, :blk]
        s_scr[rows, blk:] = _dot_nt(q, kc_ref[:, sl]) * scale + b_ref[h, :, blk:]
    for c in range(DIL_HPG * blk // rc):
        rows = slice(c * rc, (c + 1) * rc)
        s = s_scr[rows, :]
        m = jnp.max(s, axis=-1, keepdims=True)
        p_scr[rows, :] = jnp.exp(s - m).astype(p_scr.dtype)
        m_scr[rows, :] = jnp.broadcast_to(m, (rc, LANE))
    ones = jnp.ones((2 * blk, LANE), p_scr.dtype)
    lane = lax.broadcasted_iota(jnp.int32, (blk, LANE), 1)
    lse_tile = jnp.zeros((blk, LANE), F32)
    for h in range(DIL_HPG):
        sl = slice(h * DIL_DH, (h + 1) * DIL_DH)
        rows = slice(h * blk, (h + 1) * blk)
        o = (jnp.dot(p_scr[rows, :blk], vp_ref[:, sl], preferred_element_type=F32)
             + jnp.dot(p_scr[rows, blk:], vc_ref[:, sl], preferred_element_type=F32))
        l = jnp.dot(p_scr[rows, :], ones, preferred_element_type=F32)
        o_ref[:, sl] = (o / l).astype(o_ref.dtype)
        lse_tile = jnp.where(lane == h, m_scr[rows, :] + jnp.log(l), lse_tile)
    lse_ref[...] = lse_tile


def dilated_group_attention(qkv, bias, *, group):
    batch, dilation, ln, _ = qkv.shape
    hw = DIL_HPG * DIL_DH
    nb = ln // DIL_BLOCK

    def spec(col, prev):
        if prev:
            return pl.BlockSpec((None, None, DIL_BLOCK, hw), lambda b, r, n: (b, r, jnp.maximum(n - 1, 0), col))
        return pl.BlockSpec((None, None, DIL_BLOCK, hw), lambda b, r, n: (b, r, n, col))

    return pl.pallas_call(
        functools.partial(_dil_attn_body, scale=DIL_DH ** -0.5, rc=DIL_ROW_CHUNK),
        grid=(batch, dilation, nb),
        in_specs=[
            spec(0, False), spec(1, True), spec(1, False), spec(2, True), spec(2, False),
            pl.BlockSpec((None, DIL_HPG, DIL_BLOCK, 2 * DIL_BLOCK), lambda b, r, n: (jnp.minimum(n, 1), 0, 0, 0)),
        ],
        out_specs=[
            pl.BlockSpec((None, None, DIL_BLOCK, hw), lambda b, r, n: (b, r, n, 0)),
            pl.BlockSpec((None, None, DIL_BLOCK, LANE), lambda b, r, n: (b, r, n, 0)),
        ],
        out_shape=[
            jax.ShapeDtypeStruct((batch, dilation, ln, hw), BF16),
            jax.ShapeDtypeStruct((batch, dilation, ln, LANE), F32),
        ],
        scratch_shapes=[
            pltpu.VMEM((DIL_HPG * DIL_BLOCK, 2 * DIL_BLOCK), F32),
            pltpu.VMEM((DIL_HPG * DIL_BLOCK, 2 * DIL_BLOCK), BF16),
            pltpu.VMEM((DIL_HPG * DIL_BLOCK, LANE), F32),
        ],
        compiler_params=_cparams("parallel", "parallel", "arbitrary"),
        name=f"dilated_attention_g{group}",
    )(qkv, qkv, qkv, qkv, qkv, bias)


def _to_token_order(src_ref, dst_ref):
    d, cs, w = src_ref.shape
    for r in range(d):
        for c in range(w // LANE):
            dst_ref[c, pl.ds(r, cs, stride=d), :] = src_ref[r, :, c * LANE:(c + 1) * LANE].astype(dst_ref.dtype)


def _dil_combine_body(o0_ref, o1_ref, o2_ref, l0_ref, l1_ref, l2_ref, out_ref, o_scr, l_scr):
    for g, (o_ref, l_ref) in enumerate(((o1_ref, l1_ref), (o2_ref, l2_ref))):
        _to_token_order(o_ref, o_scr.at[g])
        _to_token_order(l_ref, l_scr.at[g])
    for h in range(DIL_HPG):
        sl = slice(h * DIL_DH, (h + 1) * DIL_DH)
        ls = [l0_ref[0, :, h:h + 1], l_scr[0, 0, :, h:h + 1], l_scr[1, 0, :, h:h + 1]]
        os_ = [o0_ref[0, :, sl].astype(F32), o_scr[0, h], o_scr[1, h]]
        m = jnp.maximum(jnp.maximum(ls[0], ls[1]), ls[2])
        es = [jnp.exp(x - m) for x in ls]
        den = es[0] + es[1] + es[2]
        acc = sum((e / den) * o for e, o in zip(es, os_))
        out_ref[:, sl] = acc.astype(out_ref.dtype)


def dilated_combine(os_, lses, *, seq, tm):
    batch = os_[0].shape[0]
    hw = os_[0].shape[-1]
    t = batch * seq
    per_seq = seq // tm

    def spec(arr):
        d, w = arr.shape[1], arr.shape[3]
        return pl.BlockSpec((None, d, tm // d, w), lambda i: (i // per_seq, 0, i % per_seq, 0))

    return pl.pallas_call(
        _dil_combine_body,
        grid=(t // tm,),
        in_specs=[spec(a) for a in (*os_, *lses)],
        out_specs=pl.BlockSpec((tm, hw), lambda i: (i, 0)),
        out_shape=jax.ShapeDtypeStruct((t, hw), BF16),
        scratch_shapes=[
            pltpu.VMEM((DIL_GROUPS - 1, hw // LANE, tm, LANE), F32),
            pltpu.VMEM((DIL_GROUPS - 1, 1, tm, LANE), F32),
        ],
        compiler_params=_cparams("parallel"),
        name="dilated_combine",
    )(*os_, *lses)


def _t5_bucket_np(dist):
    max_exact = REL_BUCKETS // 2
    large = max_exact + (np.log(np.maximum(dist, 1).astype(np.float32) / max_exact)
                         / math.log(REL_MAX_DIST / max_exact) * (REL_BUCKETS - max_exact)).astype(np.int32)
    large = np.minimum(large, REL_BUCKETS - 1)
    return np.where(dist < max_exact, dist, large)


def dilated_bias_tables(rel_bias):
    iq = np.arange(DIL_BLOCK)[:, None]
    jk = np.arange(2 * DIL_BLOCK)[None, :]
    dist_c = DIL_BLOCK + iq - jk
    tables = []
    for g, (window, dilation) in enumerate(DIL_PAIRS):
        span = window // dilation
        band = (dist_c >= 0) & (dist_c <= span)
        bucket = _t5_bucket_np(np.clip(dist_c, 0, None) * dilation)
        onehot = (bucket.reshape(-1, 1) == np.arange(REL_BUCKETS)[None, :]).astype(np.float32)
        heads = rel_bias[:, g * DIL_HPG:(g + 1) * DIL_HPG].astype(F32)
        bias = jnp.dot(onehot, heads, precision=lax.Precision.HIGHEST)
        bias = bias.reshape(DIL_BLOCK, 2 * DIL_BLOCK, DIL_HPG).transpose(2, 0, 1)
        general = jnp.where(band[None], bias, NEG)
        first = jnp.where((band & (jk >= DIL_BLOCK))[None], bias, NEG)
        tables.append(jnp.stack([first, general]))
    return tables


def _merge_body(a_ref, b_ref, wa_ref, wb_ref, ga_ref, gb_ref, o_ref):
    ya = jnp.dot(a_ref[...], wa_ref[...], preferred_element_type=F32)
    yb = jnp.dot(b_ref[...], wb_ref[...], preferred_element_type=F32)
    o_ref[...] = (ga_ref[...].astype(F32) * ya + gb_ref[...].astype(F32) * yb).astype(o_ref.dtype)


def merge_branches(o_a, o_b, w_a, w_b, z, *, gate_col, tm, tn):
    t, ka = o_a.shape
    kb = o_b.shape[1]
    n = w_a.shape[1]
    g0 = gate_col // tn
    return pl.pallas_call(
        _merge_body,
        grid=(t // tm, n // tn),
        in_specs=[
            pl.BlockSpec((tm, ka), lambda i, j: (i, 0)),
            pl.BlockSpec((tm, kb), lambda i, j: (i, 0)),
            pl.BlockSpec((ka, tn), lambda i, j: (0, j)),
            pl.BlockSpec((kb, tn), lambda i, j: (0, j)),
            pl.BlockSpec((tm, tn), lambda i, j: (i, g0 + j)),
            pl.BlockSpec((tm, tn), lambda i, j: (i, g0 + n // tn + j)),
        ],
        out_specs=pl.BlockSpec((tm, tn), lambda i, j: (i, j)),
        out_shape=jax.ShapeDtypeStruct((t, n), BF16),
        compiler_params=_cparams("parallel", "arbitrary"),
        name="merge_branches",
    )(o_a, o_b, w_a, w_b, z, z)


def _mm_residual_body(a_ref, w_ref, x_ref, o_ref):
    o_ref[...] = x_ref[...] + jnp.dot(a_ref[...], w_ref[...], preferred_element_type=F32)


def matmul_residual(a, w, x, *, tm, tn):
    t, k = a.shape
    n = w.shape[1]
    return pl.pallas_call(
        _mm_residual_body,
        grid=(t // tm, n // tn),
        in_specs=[
            pl.BlockSpec((tm, k), lambda i, j: (i, 0)),
            pl.BlockSpec((k, tn), lambda i, j: (0, j)),
            pl.BlockSpec((tm, tn), lambda i, j: (i, j)),
        ],
        out_specs=pl.BlockSpec((tm, tn), lambda i, j: (i, j)),
        out_shape=jax.ShapeDtypeStruct((t, n), F32),
        compiler_params=_cparams("parallel", "arbitrary"),
        name="matmul_residual",
    )(a, w, x)


def _swiglu_up(h, wg, wu):
    g = jnp.dot(h, wg, preferred_element_type=F32)
    u = jnp.dot(h, wu, preferred_element_type=F32)
    return g * _sigmoid(g) * u


def _swiglu_up_body(x_ref, g_ref, wg_ref, wu_ref, o_ref, h_scr):
    @pl.when(pl.program_id(1) == 0)
    def _():
        _norm_into(x_ref, g_ref, h_scr)

    o_ref[...] = _swiglu_up(h_scr[...], wg_ref[...], wu_ref[...]).astype(o_ref.dtype)


def swiglu_up(x, gain, w_gate, w_up, *, tm, tn):
    t, k = x.shape
    n = w_gate.shape[1]
    w_spec = pl.BlockSpec((k, tn), lambda i, j: (0, j))
    return pl.pallas_call(
        _swiglu_up_body,
        grid=(t // tm, n // tn),
        in_specs=[
            pl.BlockSpec((tm, k), lambda i, j: (i, 0)),
            pl.BlockSpec((1, k), lambda i, j: (0, 0)),
            w_spec, w_spec,
        ],
        out_specs=pl.BlockSpec((tm, tn), lambda i, j: (i, j)),
        out_shape=jax.ShapeDtypeStruct((t, n), BF16),
        scratch_shapes=[pltpu.VMEM((tm, k), BF16)],
        compiler_params=_cparams("parallel", "arbitrary"),
        name="swiglu_up",
    )(x, gain.reshape(1, k), w_gate, w_up)


def _grouped_up_body(te_ref, nu_ref, x_ref, g_ref, wg_ref, wu_ref, o_ref, h_scr):
    i = pl.program_id(0)

    @pl.when(i < nu_ref[0])
    def _():
        @pl.when(pl.program_id(1) == 0)
        def _():
            _norm_into(x_ref, g_ref, h_scr)

        o_ref[...] = _swiglu_up(h_scr[...], wg_ref[...], wu_ref[...]).astype(o_ref.dtype)

    @pl.when(i >= nu_ref[0])
    def _():
        o_ref[...] = jnp.zeros_like(o_ref)


def grouped_swiglu_up(xs, gain, w_gate, w_up, tile_expert, n_used, *, tm, tn):
    r, k = xs.shape
    n = w_gate.shape[2]
    nj = n // tn

    def row(i, nu):
        return jnp.minimum(i, nu[0] - 1)

    def col(i, j, nu):
        return jnp.where(i < nu[0], j, nj - 1)

    w_spec = pl.BlockSpec((None, k, tn), lambda i, j, te, nu: (te[i], 0, col(i, j, nu)))
    return pl.pallas_call(
        _grouped_up_body,
        grid_spec=pltpu.PrefetchScalarGridSpec(
            num_scalar_prefetch=2,
            grid=(r // tm, nj),
            in_specs=[
                pl.BlockSpec((tm, k), lambda i, j, te, nu: (row(i, nu), 0)),
                pl.BlockSpec((1, k), lambda i, j, te, nu: (0, 0)),
                w_spec, w_spec,
            ],
            out_specs=pl.BlockSpec((tm, tn), lambda i, j, te, nu: (i, j)),
            scratch_shapes=[pltpu.VMEM((tm, k), BF16)],
        ),
        out_shape=jax.ShapeDtypeStruct((r, n), BF16),
        compiler_params=_cparams("arbitrary", "arbitrary"),
        name="grouped_swiglu_up",
    )(tile_expert, n_used, xs, gain.reshape(1, k), w_gate, w_up)


def _grouped_down_body(te_ref, nu_ref, a_ref, w_ref, o_ref):
    i = pl.program_id(0)

    @pl.when(i < nu_ref[0])
    def _():
        o_ref[...] = jnp.dot(a_ref[...], w_ref[...], preferred_element_type=F32)

    @pl.when(i >= nu_ref[0])
    def _():
        o_ref[...] = jnp.zeros_like(o_ref)


def grouped_down(a, w_down, tile_expert, n_used, *, tm, tn):
    r, k = a.shape
    n = w_down.shape[2]
    nj = n // tn

    def row(i, nu):
        return jnp.minimum(i, nu[0] - 1)

    def col(i, j, nu):
        return jnp.where(i < nu[0], j, nj - 1)

    return pl.pallas_call(
        _grouped_down_body,
        grid_spec=pltpu.PrefetchScalarGridSpec(
            num_scalar_prefetch=2,
            grid=(r // tm, nj),
            in_specs=[
                pl.BlockSpec((tm, k), lambda i, j, te, nu: (row(i, nu), 0)),
                pl.BlockSpec((None, k, tn), lambda i, j, te, nu: (te[i], 0, col(i, j, nu))),
            ],
            out_specs=pl.BlockSpec((tm, tn), lambda i, j, te, nu: (i, j)),
        ),
        out_shape=jax.ShapeDtypeStruct((r, n), F32),
        compiler_params=_cparams("arbitrary", "arbitrary"),
        name="grouped_down",
    )(tile_expert, n_used, a, w_down)


def _router_body(x_ref, g_ref, wr_ref, idx_ref, wt_ref):
    h = _rms_rows(x_ref[...], g_ref[...])
    logits = lax.dot_general(wr_ref[...], h, (((1,), (1,)), ((), ())),
                             precision=lax.Precision.HIGHEST, preferred_element_type=F32)
    e = lax.broadcasted_iota(jnp.int32, logits.shape, 0)
    n_e = logits.shape[0]
    m1 = jnp.max(logits, axis=0, keepdims=True)
    i1 = jnp.min(jnp.where(logits == m1, e, n_e), axis=0, keepdims=True)
    rest = jnp.where(e == i1, -jnp.inf, logits)
    m2 = jnp.max(rest, axis=0, keepdims=True)
    i2 = jnp.min(jnp.where(rest == m2, e, n_e), axis=0, keepdims=True)
    t = jnp.exp(m2 - m1)
    idx_ref[0:1, :] = i1
    idx_ref[1:2, :] = i2
    wt_ref[0:1, :] = 1.0 / (1.0 + t)
    wt_ref[1:2, :] = t / (1.0 + t)


def moe_router(x, gain, w_router_t, *, tm):
    t, k = x.shape
    n_e = w_router_t.shape[0]
    out_spec = pl.BlockSpec((TOP_K, tm), lambda i: (0, i))
    return pl.pallas_call(
        _router_body,
        grid=(t // tm,),
        in_specs=[
            pl.BlockSpec((tm, k), lambda i: (i, 0)),
            pl.BlockSpec((1, k), lambda i: (0, 0)),
            pl.BlockSpec((n_e, k), lambda i: (0, 0)),
        ],
        out_specs=[out_spec, out_spec],
        out_shape=[jax.ShapeDtypeStruct((TOP_K, t), jnp.int32), jax.ShapeDtypeStruct((TOP_K, t), F32)],
        compiler_params=_cparams("parallel"),
        name="moe_router",
    )(x, gain.reshape(1, k), w_router_t)


def _row_copy(src_ref, o_ref, sem, src_row, dst_row):
    return pltpu.make_async_copy(src_ref.at[pl.ds(src_row, 1), :], o_ref.at[pl.ds(dst_row, 1), :], sem)


def _gather_body(idx_ref, src_ref, o_ref, sem):
    rows = o_ref.shape[0]

    def issue(r, c):
        _row_copy(src_ref, o_ref, sem, idx_ref[0, 0, r], r).start()
        return c

    lax.fori_loop(0, rows, issue, 0, unroll=8)
    pltpu.make_async_copy(src_ref.at[pl.ds(0, rows), :], o_ref, sem).wait()


def gather_rows(src, idx, *, rows):
    n = idx.shape[0]
    d = src.shape[1]
    steps = n // rows
    return pl.pallas_call(
        _gather_body,
        grid=(steps,),
        in_specs=[
            pl.BlockSpec((1, 1, rows), lambda i: (i, 0, 0), memory_space=pltpu.SMEM),
            pl.BlockSpec(memory_space=pl.ANY),
        ],
        out_specs=pl.BlockSpec((rows, d), lambda i: (i, 0)),
        out_shape=jax.ShapeDtypeStruct((n, d), src.dtype),
        scratch_shapes=[pltpu.SemaphoreType.DMA(())],
        compiler_params=_cparams("arbitrary"),
        name="gather_rows",
    )(idx.reshape(steps, 1, rows), src)


def _moe_combine_body(x_ref, y0_ref, y1_ref, w_ref, o_ref):
    w = w_ref[...]
    o_ref[...] = x_ref[...] + w[:, 0:1] * y0_ref[...] + w[:, 1:2] * y1_ref[...]


def moe_combine(x, yg, wts, *, tm):
    t, d = x.shape
    off = t // tm
    return pl.pallas_call(
        _moe_combine_body,
        grid=(t // tm,),
        in_specs=[
            pl.BlockSpec((tm, d), lambda i: (i, 0)),
            pl.BlockSpec((tm, d), lambda i: (i, 0)),
            pl.BlockSpec((tm, d), lambda i: (i + off, 0)),
            pl.BlockSpec((tm, TOP_K), lambda i: (i, 0)),
        ],
        out_specs=pl.BlockSpec((tm, d), lambda i: (i, 0)),
        out_shape=jax.ShapeDtypeStruct((t, d), F32),
        compiler_params=_cparams("parallel"),
        name="moe_combine",
    )(x, yg, yg, wts)


def moe_dispatch_plan(top_idx, *, tm):
    k, t = top_idx.shape
    n_rows = k * t + N_EXPERTS * tm
    n_tiles = n_rows // tm
    e = top_idx.reshape(-1)
    onehot = (e[:, None] == jnp.arange(N_EXPERTS, dtype=jnp.int32)[None, :]).astype(jnp.int32)
    csum = jnp.cumsum(onehot, axis=0)
    counts = csum[-1]
    rank = jnp.sum((csum - onehot) * onehot, axis=1)
    padded = ((counts + tm - 1) // tm) * tm
    ends = jnp.cumsum(padded)
    starts = ends - padded
    pos = jnp.sum(onehot * starts[None, :], axis=1) + rank
    token = jnp.tile(jnp.arange(t, dtype=jnp.int32), k)
    row_token = jnp.zeros((n_rows,), jnp.int32).at[pos].set(token)
    n_used = (ends[-1] // tm).astype(jnp.int32)
    tile_start = jnp.minimum(jnp.arange(n_tiles, dtype=jnp.int32), n_used - 1) * tm
    tile_expert = jnp.sum((tile_start[:, None] >= ends[None, :]).astype(jnp.int32), axis=1)
    return row_token, pos.astype(jnp.int32), tile_expert.astype(jnp.int32), n_used.reshape(1)


def moe_ffn(x, gain, w_router, w_gate, w_up, w_down):
    idx, wts = moe_router(x, gain, w_router.T, tm=TM_MOE)
    row_token, pos, tile_expert, n_used = moe_dispatch_plan(idx, tm=TM_MOE)
    xs = gather_rows(x, row_token, rows=GATHER_ROWS)
    a = grouped_swiglu_up(xs, gain, w_gate, w_up, tile_expert, n_used, tm=TM_MOE, tn=TN)
    y = grouped_down(a, w_down, tile_expert, n_used, tm=TM_MOE, tn=TN)
    yg = gather_rows(y, pos, rows=GATHER_ROWS)
    return moe_combine(x, yg, wts.T, tm=TM_MOE)


def _rmsnorm_body(x_ref, g_ref, o_ref):
    o_ref[...] = _rms_rows(x_ref[...], g_ref[...])


def rmsnorm(x, gain, *, tm):
    t, k = x.shape
    return pl.pallas_call(
        _rmsnorm_body,
        grid=(t // tm,),
        in_specs=[pl.BlockSpec((tm, k), lambda i: (i, 0)), pl.BlockSpec((1, k), lambda i: (0, 0))],
        out_specs=pl.BlockSpec((tm, k), lambda i: (i, 0)),
        out_shape=jax.ShapeDtypeStruct((t, k), F32),
        compiler_params=_cparams("parallel"),
        name="final_rmsnorm",
    )(x, gain.reshape(1, k))


def _rope_tables(seq, scale):
    half = MLA_ROPE // 2
    inv = ROPE_THETA ** (-jnp.arange(half, dtype=F32) / half)
    ang = jnp.arange(seq).astype(F32)[:, None] * inv[None, :]
    cos, sin = jnp.cos(ang) * scale, jnp.sin(ang) * scale
    z = jnp.zeros_like(cos)
    cat = lambda *p: jnp.concatenate(p, axis=1)
    return cat(cos, cos, z, z), cat(-sin, z, z, z), cat(z, sin, z, z)


def _layout_w_in(w, d_model):
    off_kva = MLA_Q_RANK
    off_dil = off_kva + MLA_KV_RANK + MLA_ROPE
    col_dil = 3 * DIL_HEADS * DIL_DH
    off_gate = off_dil + col_dil
    pad = jnp.zeros((w.shape[0], 2 * LANE - MLA_ROPE), w.dtype)
    small = jnp.concatenate([w[:, :off_dil], pad, w[:, off_gate:]], axis=1)
    return small.astype(BF16), w[:, off_dil:off_gate].astype(BF16)


def _layout_w_uq(w):
    r = w.shape[0]
    w = w.reshape(r, MLA_HEADS, MLA_NOPE + MLA_ROPE)
    w = jnp.pad(w, ((0, 0), (0, 0), (0, MLA_QK - MLA_NOPE - MLA_ROPE)))
    return w.reshape(r, MLA_HEADS * MLA_QK).astype(BF16)


def _layout_w_ukv(w):
    r = w.shape[0]
    w = w.reshape(r, MLA_HEADS, MLA_NOPE + MLA_V)
    k = w[:, :, :MLA_NOPE].reshape(r, MLA_HEADS * MLA_NOPE)
    v = w[:, :, MLA_NOPE:].reshape(r, MLA_HEADS * MLA_V)
    return jnp.concatenate([k, v], axis=1).astype(BF16)


def kernel(x, mix_norm, w_in, q_norm, w_uq, kv_norm, w_ukv, w_o_mla, w_o_dil, w_out, rel_bias,
           ffn_norm, w_ffn_gate, w_ffn_up, w_ffn_down, w_router, w_exp_gate, w_exp_up, w_exp_down,
           final_norm):
    batch, seq, d_model = x.shape
    depth = w_in.shape[0]
    t = batch * seq
    x = x.reshape(t, d_model)

    q_scale = (MLA_NOPE + MLA_ROPE) ** -0.5 * math.log2(math.e)
    q_tabs = _rope_tables(seq, q_scale)
    k_tabs = _rope_tables(seq, 1.0)
    bias_tabs = dilated_bias_tables(rel_bias)
    gate_col = MLA_Q_RANK + MLA_KV_RANK + 2 * LANE

    for layer in range(depth):
        w_small, w_dil = _layout_w_in(w_in[layer], d_model)
        z = norm_matmul(x, mix_norm[layer], w_small, tm=TM, tn=TN, sig_from=gate_col // TN)
        qkvs = dilated_qkv_proj(x, mix_norm[layer], w_dil, batch=batch, seq=seq, tm=TM, tn=TN)

        q = q_proj(z, q_norm[layer], _layout_w_uq(w_uq[layer]), q_tabs, seq=seq, tm=TQ, scale=q_scale)
        k, v = kv_proj(z, kv_norm[layer], _layout_w_ukv(w_ukv[layer]), k_tabs, seq=seq, tm=TQ)
        o_mla = mla_attention(q, k, v, batch=batch, seq=seq, tq=TQ)

        os_, lses = [], []
        for g in range(DIL_GROUPS):
            o_g, lse_g = dilated_group_attention(qkvs[g], bias_tabs[g], group=g)
            os_.append(o_g)
            lses.append(lse_g)
        o_dil = dilated_combine(os_, lses, seq=seq, tm=TM_COMBINE)

        merged = merge_branches(o_mla, o_dil, w_o_mla[layer].astype(BF16), w_o_dil[layer].astype(BF16),
                                z, gate_col=gate_col, tm=TM, tn=TN)
        x = matmul_residual(merged, w_out[layer].astype(BF16), x, tm=TM, tn=TN)

        i = layer // 2
        if layer % 2 == 0:
            a = swiglu_up(x, ffn_norm[layer], w_ffn_gate[i].astype(BF16), w_ffn_up[i].astype(BF16),
                          tm=TM, tn=TN)
            x = matmul_residual(a, w_ffn_down[i].astype(BF16), x, tm=TM_DOWN, tn=TN)
        else:
            x = moe_ffn(x, ffn_norm[layer], w_router[i], w_exp_gate[i].astype(BF16),
                        w_exp_up[i].astype(BF16), w_exp_down[i].astype(BF16))

    return rmsnorm(x, final_norm, tm=TM).reshape(batch, seq, d_model)
```

```python
import functools
import math

import numpy as np
import jax
import jax.numpy as jnp
from jax import lax
from jax.experimental import pallas as pl
from jax.experimental.pallas import tpu as pltpu

F32 = jnp.float32
BF16 = jnp.bfloat16

EPS = 1e-6
NEG = -1e30

MLA_HEADS = 16
MLA_Q_RANK = 512
MLA_KV_RANK = 256
MLA_NOPE = 128
MLA_ROPE = 64
MLA_V = 128
ROPE_THETA = 10000.0
DIL_PAIRS = ((128, 1), (512, 4), (2048, 16))
DIL_GROUPS = 3
DIL_HPG = 8
DIL_HEADS = DIL_GROUPS * DIL_HPG
DIL_DH = 128
DIL_BLOCK = 128
REL_BUCKETS = 32
REL_MAX_DIST = 2048
N_EXPERTS = 8
TOP_K = 2

LANE = 128
MLA_QK = 256
VMEM_LIMIT = 56 * 2**20

TM = 1024
TM_DOWN = 512
TN = 512
TN_WIDE = 1024
TN_FF = 1408
TM_MOE = 512
TQ = 512
TQ_ATT = 2048
MLA_ROW_RUN = 256
DIL_ROW_CHUNK = 64
TM_COMBINE = 512
GATHER_ROWS = 256
NORM_CHUNK = 256


def _cparams(*sem):
    return pltpu.CompilerParams(dimension_semantics=sem, vmem_limit_bytes=VMEM_LIMIT)


def _rms_rows(x, g):
    ms = jnp.mean(x * x, axis=-1, keepdims=True)
    return x * lax.rsqrt(ms + EPS) * g


def _norm_into(x_ref, g_ref, h_scr):
    rows = x_ref.shape[0]
    chunk = min(NORM_CHUNK, rows)
    for r in range(0, rows, chunk):
        x = x_ref[r:r + chunk, :].astype(F32)
        h_scr[r:r + chunk, :] = _rms_rows(x, g_ref[...]).astype(h_scr.dtype)


def _sigmoid(x):
    return 1.0 / (1.0 + jnp.exp(-x))


def _norm_mm_body(x_ref, g_ref, w_ref, o_ref, h_scr, *, sig_from):
    j = pl.program_id(1)

    @pl.when(j == 0)
    def _():
        _norm_into(x_ref, g_ref, h_scr)

    acc = jnp.dot(h_scr[...], w_ref[...], preferred_element_type=F32)
    if sig_from is None:
        o_ref[...] = acc.astype(o_ref.dtype)
    else:
        @pl.when(j < sig_from)
        def _():
            o_ref[...] = acc.astype(o_ref.dtype)

        @pl.when(j >= sig_from)
        def _():
            o_ref[...] = _sigmoid(acc).astype(o_ref.dtype)


def norm_matmul(x, gain, w, *, tm, tn, sig_from=None):
    t, k = x.shape
    n = w.shape[1]
    return pl.pallas_call(
        functools.partial(_norm_mm_body, sig_from=sig_from),
        grid=(t // tm, n // tn),
        in_specs=[
            pl.BlockSpec((tm, k), lambda i, j: (i, 0)),
            pl.BlockSpec((1, k), lambda i, j: (0, 0)),
            pl.BlockSpec((k, tn), lambda i, j: (0, j)),
        ],
        out_specs=pl.BlockSpec((tm, tn), lambda i, j: (i, j)),
        out_shape=jax.ShapeDtypeStruct((t, n), BF16),
        scratch_shapes=[pltpu.VMEM((tm, k), BF16)],
        compiler_params=_cparams("parallel", "arbitrary"),
        name="norm_matmul",
    )(x, gain.reshape(1, k), w)


def _dil_proj_body(x_ref, g_ref, w_ref, o0_ref, o1_ref, o2_ref, h_scr, hf_scr, *, tiles_per_group):
    j = pl.program_id(1)
    tm, k = x_ref.shape
    o_refs = (o0_ref, o1_ref, o2_ref)

    @pl.when(j == 0)
    def _():
        chunk = min(NORM_CHUNK, tm)
        for r0 in range(0, tm, chunk):
            h = _rms_rows(x_ref[r0:r0 + chunk, :], g_ref[...])
            for c in range(k // LANE):
                hf_scr[c, r0:r0 + chunk, :] = h[:, c * LANE:(c + 1) * LANE]
        for g, o_ref in enumerate(o_refs):
            d = o_ref.shape[0]
            cs = tm // d
            for r in range(d):
                for c in range(k // LANE):
                    rows = hf_scr[c, pl.ds(r, cs, stride=d), :] if d > 1 else hf_scr[c]
                    h_scr[g, r * cs:(r + 1) * cs, c * LANE:(c + 1) * LANE] = rows.astype(h_scr.dtype)

    group = j // tiles_per_group
    acc = jnp.dot(h_scr[group], w_ref[...], preferred_element_type=F32)
    for g, o_ref in enumerate(o_refs):
        d = o_ref.shape[0]
        cs = tm // d

        @pl.when(group == g)
        def _(o_ref=o_ref, d=d, cs=cs):
            for r in range(d):
                o_ref[r] = acc[r * cs:(r + 1) * cs, :].astype(o_ref.dtype)


def dilated_qkv_proj(x, gain, w, *, batch, seq, tm, tn):
    t, k = x.shape
    unit = DIL_HPG * DIL_DH
    upt = unit // tn
    tpg = 3 * upt
    per_seq = seq // tm

    def w_col(j):
        return ((j % tpg) // upt) * (DIL_GROUPS * upt) + (j // tpg) * upt + j % upt

    out_specs, out_shape = [], []
    for g, (_, d) in enumerate(DIL_PAIRS):
        out_specs.append(pl.BlockSpec(
            (None, d, tm // d, tn),
            lambda i, j, g=g: (i // per_seq, 0, i % per_seq, jnp.clip(j - g * tpg, 0, tpg - 1))))
        out_shape.append(jax.ShapeDtypeStruct((batch, d, seq // d, 3 * unit), BF16))
    return pl.pallas_call(
        functools.partial(_dil_proj_body, tiles_per_group=tpg),
        grid=(t // tm, DIL_GROUPS * tpg),
        in_specs=[
            pl.BlockSpec((tm, k), lambda i, j: (i, 0)),
            pl.BlockSpec((1, k), lambda i, j: (0, 0)),
            pl.BlockSpec((k, tn), lambda i, j: (0, w_col(j))),
        ],
        out_specs=out_specs,
        out_shape=out_shape,
        scratch_shapes=[pltpu.VMEM((DIL_GROUPS, tm, k), BF16), pltpu.VMEM((k // LANE, tm, LANE), F32)],
        compiler_params=_cparams("parallel", "arbitrary"),
        name="dilated_qkv_proj",
    )(x, gain.reshape(1, k), w)


def _rope128(pe, c_ref, sa_ref, sb_ref):
    return (pe * c_ref[...]
            + pltpu.roll(pe, 96, 1) * sa_ref[...]
            + pltpu.roll(pe, 32, 1) * sb_ref[...])


def _q_proj_body(qa_ref, g_ref, w_ref, c_ref, sa_ref, sb_ref, o_ref, h_scr, *, scale):
    _norm_into(qa_ref, g_ref, h_scr)
    for h in range(MLA_HEADS):
        acc = jnp.dot(h_scr[...], w_ref[:, h * MLA_QK:(h + 1) * MLA_QK], preferred_element_type=F32)
        o_ref[:, h * MLA_QK:h * MLA_QK + LANE] = (acc[:, :LANE] * scale).astype(o_ref.dtype)
        pe = _rope128(acc[:, LANE:], c_ref, sa_ref, sb_ref)
        o_ref[:, h * MLA_QK + LANE:(h + 1) * MLA_QK] = pe.astype(o_ref.dtype)


def q_proj(z, gain, w, tabs, *, seq, tm, scale):
    t = z.shape[0]
    r = w.shape[0]
    n = w.shape[1]
    per_seq = seq // tm
    tab_spec = pl.BlockSpec((tm, LANE), lambda i: (i % per_seq, 0))
    return pl.pallas_call(
        functools.partial(_q_proj_body, scale=scale),
        grid=(t // tm,),
        in_specs=[
            pl.BlockSpec((tm, r), lambda i: (i, 0)),
            pl.BlockSpec((1, r), lambda i: (0, 0)),
            pl.BlockSpec((r, n), lambda i: (0, 0)),
            tab_spec, tab_spec, tab_spec,
        ],
        out_specs=pl.BlockSpec((tm, n), lambda i: (i, 0)),
        out_shape=jax.ShapeDtypeStruct((t, n), BF16),
        scratch_shapes=[pltpu.VMEM((tm, r), BF16)],
        compiler_params=_cparams("parallel"),
        name="mla_q_proj",
    )(z, gain.reshape(1, r), w, *tabs)


def _kv_proj_body(ckv_ref, kpe_ref, g_ref, w_ref, c_ref, sa_ref, sb_ref, k_ref, v_ref, h_scr):
    _norm_into(ckv_ref, g_ref, h_scr)
    pe = _rope128(kpe_ref[...].astype(F32), c_ref, sa_ref, sb_ref).astype(k_ref.dtype)
    nk = MLA_HEADS * MLA_NOPE
    for h in range(MLA_HEADS):
        kn = jnp.dot(h_scr[...], w_ref[:, h * MLA_NOPE:(h + 1) * MLA_NOPE], preferred_element_type=F32)
        k_ref[:, h * MLA_QK:h * MLA_QK + LANE] = kn.astype(k_ref.dtype)
        k_ref[:, h * MLA_QK + LANE:(h + 1) * MLA_QK] = pe
    v = jnp.dot(h_scr[...], w_ref[:, nk:], preferred_element_type=F32)
    v_ref[...] = v.astype(v_ref.dtype)


def kv_proj(z, gain, w, tabs, *, seq, tm):
    t = z.shape[0]
    r = w.shape[0]
    per_seq = seq // tm
    nv = MLA_HEADS * MLA_V
    tab_spec = pl.BlockSpec((tm, LANE), lambda i: (i % per_seq, 0))
    return pl.pallas_call(
        _kv_proj_body,
        grid=(t // tm,),
        in_specs=[
            pl.BlockSpec((tm, r), lambda i: (i, MLA_Q_RANK // r)),
            pl.BlockSpec((tm, LANE), lambda i: (i, (MLA_Q_RANK + r) // LANE)),
            pl.BlockSpec((1, r), lambda i: (0, 0)),
            pl.BlockSpec(w.shape, lambda i: (0, 0)),
            tab_spec, tab_spec, tab_spec,
        ],
        out_specs=[
            pl.BlockSpec((tm, MLA_HEADS * MLA_QK), lambda i: (i, 0)),
            pl.BlockSpec((tm, nv), lambda i: (i, 0)),
        ],
        out_shape=[
            jax.ShapeDtypeStruct((t, MLA_HEADS * MLA_QK), BF16),
            jax.ShapeDtypeStruct((t, nv), BF16),
        ],
        scratch_shapes=[pltpu.VMEM((tm, r), BF16)],
        compiler_params=_cparams("parallel"),
        name="mla_kv_proj",
    )(z, z, gain.reshape(1, r), w, *tabs)


def _dot_nt(a, b):
    return lax.dot_general(a, b, (((1,), (1,)), ((), ())), preferred_element_type=F32)


def _mla_attn_body(q_ref, k_ref, v_ref, o_ref, *, tq, tr):
    qi = pl.program_id(2)
    row = lax.broadcasted_iota(jnp.int32, (tr, tr), 0)
    col = lax.broadcasted_iota(jnp.int32, (tr, tr), 1)

    def run(first):
        rows = slice(first % tq, first % tq + tr)
        q = q_ref[rows, :]
        s_d = jnp.where(col <= row, _dot_nt(q, k_ref[first:first + tr, :]), NEG)
        m = jnp.max(s_d, axis=-1, keepdims=True)
        if first:
            s_o = _dot_nt(q, k_ref[:first, :])
            m = jnp.maximum(m, jnp.max(s_o, axis=-1, keepdims=True))
        p_d = jnp.exp2(s_d - m)
        l = jnp.sum(p_d, axis=-1, keepdims=True)
        o = jnp.dot(p_d.astype(v_ref.dtype), v_ref[first:first + tr, :], preferred_element_type=F32)
        if first:
            p_o = jnp.exp2(s_o - m)
            l = l + jnp.sum(p_o, axis=-1, keepdims=True)
            o = o + jnp.dot(p_o.astype(v_ref.dtype), v_ref[:first, :], preferred_element_type=F32)
        o_ref[rows, :] = (o / l).astype(o_ref.dtype)

    for n in range(k_ref.shape[0] // tq):
        @pl.when(qi == n)
        def _(n=n):
            for first in range(n * tq, (n + 1) * tq, tr):
                run(first)


def mla_attention(q, k, v, *, batch, seq, tq):
    t = batch * seq
    q3 = q.reshape(batch, seq, MLA_HEADS * MLA_QK)
    k3 = k.reshape(batch, seq, MLA_HEADS * MLA_QK)
    v3 = v.reshape(batch, seq, MLA_HEADS * MLA_V)
    out = pl.pallas_call(
        functools.partial(_mla_attn_body, tq=tq, tr=MLA_ROW_RUN),
        grid=(batch, MLA_HEADS, seq // tq),
        in_specs=[
            pl.BlockSpec((None, tq, MLA_QK), lambda b, h, i: (b, i, h)),
            pl.BlockSpec((None, seq, MLA_QK), lambda b, h, i: (b, 0, h)),
            pl.BlockSpec((None, seq, MLA_V), lambda b, h, i: (b, 0, h)),
        ],
        out_specs=pl.BlockSpec((None, tq, MLA_V), lambda b, h, i: (b, i, h)),
        out_shape=jax.ShapeDtypeStruct((batch, seq, MLA_HEADS * MLA_V), BF16),
        compiler_params=_cparams("parallel", "parallel", "arbitrary"),
        name="mla_attention",
    )(q3, k3, v3)
    return out.reshape(t, MLA_HEADS * MLA_V)


def _dil_attn_body(q_ref, kp_ref, kc_ref, vp_ref, vc_ref, b_ref, o_ref, lse_ref, s_scr, p_scr, m_scr,
                   *, scale, rc):
    blk = q_ref.shape[0]
    for h in range(DIL_HPG):
        sl = slice(h * DIL_DH, (h + 1) * DIL_DH)
        rows = slice(h * blk, (h + 1) * blk)
        q = q_ref[:, sl]
        s_scr[rows, :blk] = _dot_nt(q, kp_ref[:, sl]) * scale + b_ref[h, :, :blk]
        s_scr[rows, blk:] = _dot_nt(q, kc_ref[:, sl]) * scale + b_ref[h, :, blk:]
    for c in range(DIL_HPG * blk // rc):
        rows = slice(c * rc, (c + 1) * rc)
        s = s_scr[rows, :]
        m = jnp.max(s, axis=-1, keepdims=True)
        p_scr[rows, :] = jnp.exp(s - m).astype(p_scr.dtype)
        m_scr[rows, :] = jnp.broadcast_to(m, (rc, LANE))
    ones = jnp.ones((2 * blk, LANE), p_scr.dtype)
    lane = lax.broadcasted_iota(jnp.int32, (blk, LANE), 1)
    lse_tile = jnp.zeros((blk, LANE), F32)
    for h in range(DIL_HPG):
        sl = slice(h * DIL_DH, (h + 1) * DIL_DH)
        rows = slice(h * blk, (h + 1) * blk)
        o = (jnp.dot(p_scr[rows, :blk], vp_ref[:, sl], preferred_element_type=F32)
             + jnp.dot(p_scr[rows, blk:], vc_ref[:, sl], preferred_element_type=F32))
        l = jnp.dot(p_scr[rows, :], ones, preferred_element_type=F32)
        o_ref[:, sl] = (o / l).astype(o_ref.dtype)
        lse_tile = jnp.where(lane == h, m_scr[rows, :] + jnp.log(l), lse_tile)
    lse_ref[...] = lse_tile


def dilated_group_attention(qkv, bias, *, group):
    batch, dilation, ln, _ = qkv.shape
    hw = DIL_HPG * DIL_DH
    nb = ln // DIL_BLOCK

    def spec(col, prev):
        if prev:
            return pl.BlockSpec((None, None, DIL_BLOCK, hw), lambda b, r, n: (b, r, jnp.maximum(n - 1, 0), col))
        return pl.BlockSpec((None, None, DIL_BLOCK, hw), lambda b, r, n: (b, r, n, col))

    return pl.pallas_call(
        functools.partial(_dil_attn_body, scale=DIL_DH ** -0.5, rc=DIL_ROW_CHUNK),
        grid=(batch, dilation, nb),
        in_specs=[
            spec(0, False), spec(1, True), spec(1, False), spec(2, True), spec(2, False),
            pl.BlockSpec((None, DIL_HPG, DIL_BLOCK, 2 * DIL_BLOCK), lambda b, r, n: (jnp.minimum(n, 1), 0, 0, 0)),
        ],
        out_specs=[
            pl.BlockSpec((None, None, DIL_BLOCK, hw), lambda b, r, n: (b, r, n, 0)),
            pl.BlockSpec((None, None, DIL_BLOCK, LANE), lambda b, r, n: (b, r, n, 0)),
        ],
        out_shape=[
            jax.ShapeDtypeStruct((batch, dilation, ln, hw), BF16),
            jax.ShapeDtypeStruct((batch, dilation, ln, LANE), F32),
        ],
        scratch_shapes=[
            pltpu.VMEM((DIL_HPG * DIL_BLOCK, 2 * DIL_BLOCK), F32),
            pltpu.VMEM((DIL_HPG * DIL_BLOCK, 2 * DIL_BLOCK), BF16),
            pltpu.VMEM((DIL_HPG * DIL_BLOCK, LANE), F32),
        ],
        compiler_params=_cparams("parallel", "parallel", "arbitrary"),
        name=f"dilated_attention_g{group}",
    )(qkv, qkv, qkv, qkv, qkv, bias)


def _to_token_order(src_ref, dst_ref):
    d, cs, w = src_ref.shape
    for r in range(d):
        for c in range(w // LANE):
            dst_ref[c, pl.ds(r, cs, stride=d), :] = src_ref[r, :, c * LANE:(c + 1) * LANE].astype(dst_ref.dtype)


def _dil_combine_body(o0_ref, o1_ref, o2_ref, l0_ref, l1_ref, l2_ref, out_ref, o_scr, l_scr):
    for g, (o_ref, l_ref) in enumerate(((o1_ref, l1_ref), (o2_ref, l2_ref))):
        _to_token_order(o_ref, o_scr.at[g])
        _to_token_order(l_ref, l_scr.at[g])
    for h in range(DIL_HPG):
        sl = slice(h * DIL_DH, (h + 1) * DIL_DH)
        ls = [l0_ref[0, :, h:h + 1], l_scr[0, 0, :, h:h + 1], l_scr[1, 0, :, h:h + 1]]
        os_ = [o0_ref[0, :, sl].astype(F32), o_scr[0, h], o_scr[1, h]]
        m = jnp.maximum(jnp.maximum(ls[0], ls[1]), ls[2])
        es = [jnp.exp(x - m) for x in ls]
        den = es[0] + es[1] + es[2]
        acc = sum((e / den) * o for e, o in zip(es, os_))
        out_ref[:, sl] = acc.astype(out_ref.dtype)


def dilated_combine(os_, lses, *, seq, tm):
    batch = os_[0].shape[0]
    hw = os_[0].shape[-1]
    t = batch * seq
    per_seq = seq // tm

    def spec(arr):
        d, w = arr.shape[1], arr.shape[3]
        return pl.BlockSpec((None, d, tm // d, w), lambda i: (i // per_seq, 0, i % per_seq, 0))

    return pl.pallas_call(
        _dil_combine_body,
        grid=(t // tm,),
        in_specs=[spec(a) for a in (*os_, *lses)],
        out_specs=pl.BlockSpec((tm, hw), lambda i: (i, 0)),
        out_shape=jax.ShapeDtypeStruct((t, hw), BF16),
        scratch_shapes=[
            pltpu.VMEM((DIL_GROUPS - 1, hw // LANE, tm, LANE), F32),
            pltpu.VMEM((DIL_GROUPS - 1, 1, tm, LANE), F32),
        ],
        compiler_params=_cparams("parallel"),
        name="dilated_combine",
    )(*os_, *lses)


def _t5_bucket_np(dist):
    max_exact = REL_BUCKETS // 2
    large = max_exact + (np.log(np.maximum(dist, 1).astype(np.float32) / max_exact)
                         / math.log(REL_MAX_DIST / max_exact) * (REL_BUCKETS - max_exact)).astype(np.int32)
    large = np.minimum(large, REL_BUCKETS - 1)
    return np.where(dist < max_exact, dist, large)


def dilated_bias_tables(rel_bias):
    iq = np.arange(DIL_BLOCK)[:, None]
    jk = np.arange(2 * DIL_BLOCK)[None, :]
    dist_c = DIL_BLOCK + iq - jk
    tables = []
    for g, (window, dilation) in enumerate(DIL_PAIRS):
        span = window // dilation
        band = (dist_c >= 0) & (dist_c <= span)
        bucket = _t5_bucket_np(np.clip(dist_c, 0, None) * dilation)
        onehot = (bucket.reshape(-1, 1) == np.arange(REL_BUCKETS)[None, :]).astype(np.float32)
        heads = rel_bias[:, g * DIL_HPG:(g + 1) * DIL_HPG].astype(F32)
        bias = jnp.dot(onehot, heads, precision=lax.Precision.HIGHEST)
        bias = bias.reshape(DIL_BLOCK, 2 * DIL_BLOCK, DIL_HPG).transpose(2, 0, 1)
        general = jnp.where(band[None], bias, NEG)
        first = jnp.where((band & (jk >= DIL_BLOCK))[None], bias, NEG)
        tables.append(jnp.stack([first, general]))
    return tables


def _merge_body(a_ref, b_ref, wa_ref, wb_ref, ga_ref, gb_ref, o_ref):
    ya = jnp.dot(a_ref[...], wa_ref[...], preferred_element_type=F32)
    yb = jnp.dot(b_ref[...], wb_ref[...], preferred_element_type=F32)
    o_ref[...] = (ga_ref[...].astype(F32) * ya + gb_ref[...].astype(F32) * yb).astype(o_ref.dtype)


def merge_branches(o_a, o_b, w_a, w_b, z, *, gate_col, tm, tn):
    t, ka = o_a.shape
    kb = o_b.shape[1]
    n = w_a.shape[1]
    g0 = gate_col // tn
    return pl.pallas_call(
        _merge_body,
        grid=(t // tm, n // tn),
        in_specs=[
            pl.BlockSpec((tm, ka), lambda i, j: (i, 0)),
            pl.BlockSpec((tm, kb), lambda i, j: (i, 0)),
            pl.BlockSpec((ka, tn), lambda i, j: (0, j)),
            pl.BlockSpec((kb, tn), lambda i, j: (0, j)),
            pl.BlockSpec((tm, tn), lambda i, j: (i, g0 + j)),
            pl.BlockSpec((tm, tn), lambda i, j: (i, g0 + n // tn + j)),
        ],
        out_specs=pl.BlockSpec((tm, tn), lambda i, j: (i, j)),
        out_shape=jax.ShapeDtypeStruct((t, n), BF16),
        compiler_params=_cparams("parallel", "arbitrary"),
        name="merge_branches",
    )(o_a, o_b, w_a, w_b, z, z)


def _mm_residual_body(a_ref, w_ref, x_ref, o_ref):
    o_ref[...] = x_ref[...] + jnp.dot(a_ref[...], w_ref[...], preferred_element_type=F32)


def matmul_residual(a, w, x, *, tm, tn):
    t, k = a.shape
    n = w.shape[1]
    return pl.pallas_call(
        _mm_residual_body,
        grid=(t // tm, n // tn),
        in_specs=[
            pl.BlockSpec((tm, k), lambda i, j: (i, 0)),
            pl.BlockSpec((k, tn), lambda i, j: (0, j)),
            pl.BlockSpec((tm, tn), lambda i, j: (i, j)),
        ],
        out_specs=pl.BlockSpec((tm, tn), lambda i, j: (i, j)),
        out_shape=jax.ShapeDtypeStruct((t, n), F32),
        compiler_params=_cparams("parallel", "arbitrary"),
        name="matmul_residual",
    )(a, w, x)


def _swiglu_up(h, wg, wu):
    g = jnp.dot(h, wg, preferred_element_type=F32)
    u = jnp.dot(h, wu, preferred_element_type=F32)
    return g * _sigmoid(g) * u


def _swiglu_up_body(x_ref, g_ref, wg_ref, wu_ref, o_ref, h_scr):
    @pl.when(pl.program_id(1) == 0)
    def _():
        _norm_into(x_ref, g_ref, h_scr)

    o_ref[...] = _swiglu_up(h_scr[...], wg_ref[...], wu_ref[...]).astype(o_ref.dtype)


def swiglu_up(x, gain, w_gate, w_up, *, tm, tn):
    t, k = x.shape
    n = w_gate.shape[1]
    w_spec = pl.BlockSpec((k, tn), lambda i, j: (0, j))
    return pl.pallas_call(
        _swiglu_up_body,
        grid=(t // tm, n // tn),
        in_specs=[
            pl.BlockSpec((tm, k), lambda i, j: (i, 0)),
            pl.BlockSpec((1, k), lambda i, j: (0, 0)),
            w_spec, w_spec,
        ],
        out_specs=pl.BlockSpec((tm, tn), lambda i, j: (i, j)),
        out_shape=jax.ShapeDtypeStruct((t, n), BF16),
        scratch_shapes=[pltpu.VMEM((tm, k), BF16)],
        compiler_params=_cparams("parallel", "arbitrary"),
        name="swiglu_up",
    )(x, gain.reshape(1, k), w_gate, w_up)


def _grouped_up_body(te_ref, nu_ref, h_ref, wg_ref, wu_ref, o_ref):
    i = pl.program_id(0)

    @pl.when(i < nu_ref[0])
    def _():
        o_ref[...] = _swiglu_up(h_ref[...], wg_ref[...], wu_ref[...]).astype(o_ref.dtype)

    @pl.when(i >= nu_ref[0])
    def _():
        o_ref[...] = jnp.zeros_like(o_ref)


def grouped_swiglu_up(hs, w_gate, w_up, tile_expert, n_used, *, tm, tn):
    r, k = hs.shape
    n = w_gate.shape[2]
    nj = n // tn

    def row(i, nu):
        return jnp.minimum(i, nu[0] - 1)

    def col(i, j, nu):
        return jnp.where(i < nu[0], j, nj - 1)

    w_spec = pl.BlockSpec((None, k, tn), lambda i, j, te, nu: (te[i], 0, col(i, j, nu)))
    return pl.pallas_call(
        _grouped_up_body,
        grid_spec=pltpu.PrefetchScalarGridSpec(
            num_scalar_prefetch=2,
            grid=(r // tm, nj),
            in_specs=[
                pl.BlockSpec((tm, k), lambda i, j, te, nu: (row(i, nu), 0)),
                w_spec, w_spec,
            ],
            out_specs=pl.BlockSpec((tm, tn), lambda i, j, te, nu: (i, j)),
        ),
        out_shape=jax.ShapeDtypeStruct((r, n), BF16),
        compiler_params=_cparams("arbitrary", "arbitrary"),
        name="grouped_swiglu_up",
    )(tile_expert, n_used, hs, w_gate, w_up)


def _grouped_down_body(te_ref, nu_ref, a_ref, w_ref, o_ref):
    i = pl.program_id(0)

    @pl.when(i < nu_ref[0])
    def _():
        o_ref[...] = jnp.dot(a_ref[...], w_ref[...], preferred_element_type=F32)

    @pl.when(i >= nu_ref[0])
    def _():
        o_ref[...] = jnp.zeros_like(o_ref)


def grouped_down(a, w_down, tile_expert, n_used, *, tm, tn):
    r, k = a.shape
    n = w_down.shape[2]
    nj = n // tn

    def row(i, nu):
        return jnp.minimum(i, nu[0] - 1)

    def col(i, j, nu):
        return jnp.where(i < nu[0], j, nj - 1)

    return pl.pallas_call(
        _grouped_down_body,
        grid_spec=pltpu.PrefetchScalarGridSpec(
            num_scalar_prefetch=2,
            grid=(r // tm, nj),
            in_specs=[
                pl.BlockSpec((tm, k), lambda i, j, te, nu: (row(i, nu), 0)),
                pl.BlockSpec((None, k, tn), lambda i, j, te, nu: (te[i], 0, col(i, j, nu))),
            ],
            out_specs=pl.BlockSpec((tm, tn), lambda i, j, te, nu: (i, j)),
        ),
        out_shape=jax.ShapeDtypeStruct((r, n), F32),
        compiler_params=_cparams("arbitrary", "arbitrary"),
        name="grouped_down",
    )(tile_expert, n_used, a, w_down)


def _router_body(x_ref, g_ref, wr_ref, idx_ref, wt_ref):
    h = _rms_rows(x_ref[...], g_ref[...])
    logits = lax.dot_general(wr_ref[...], h, (((1,), (1,)), ((), ())),
                             precision=lax.Precision.HIGHEST, preferred_element_type=F32)
    e = lax.broadcasted_iota(jnp.int32, logits.shape, 0)
    n_e = logits.shape[0]
    m1 = jnp.max(logits, axis=0, keepdims=True)
    i1 = jnp.min(jnp.where(logits == m1, e, n_e), axis=0, keepdims=True)
    rest = jnp.where(e == i1, -jnp.inf, logits)
    m2 = jnp.max(rest, axis=0, keepdims=True)
    i2 = jnp.min(jnp.where(rest == m2, e, n_e), axis=0, keepdims=True)
    t = jnp.exp(m2 - m1)
    idx_ref[0:1, :] = i1
    idx_ref[1:2, :] = i2
    wt_ref[0:1, :] = 1.0 / (1.0 + t)
    wt_ref[1:2, :] = t / (1.0 + t)


def moe_router(x, gain, w_router_t, *, tm):
    t, k = x.shape
    n_e = w_router_t.shape[0]
    out_spec = pl.BlockSpec((TOP_K, tm), lambda i: (0, i))
    return pl.pallas_call(
        _router_body,
        grid=(t // tm,),
        in_specs=[
            pl.BlockSpec((tm, k), lambda i: (i, 0)),
            pl.BlockSpec((1, k), lambda i: (0, 0)),
            pl.BlockSpec((n_e, k), lambda i: (0, 0)),
        ],
        out_specs=[out_spec, out_spec],
        out_shape=[jax.ShapeDtypeStruct((TOP_K, t), jnp.int32), jax.ShapeDtypeStruct((TOP_K, t), F32)],
        compiler_params=_cparams("parallel"),
        name="moe_router",
    )(x, gain.reshape(1, k), w_router_t)


def _start_row_gather(idx_ref, src_ref, dst_ref, sem):
    def issue(r, c):
        pltpu.make_async_copy(src_ref.at[pl.ds(idx_ref[0, 0, r], 1), :], dst_ref.at[pl.ds(r, 1), :], sem).start()
        return c

    lax.fori_loop(0, dst_ref.shape[0], issue, 0, unroll=8)


def _wait_row_gather(src_ref, dst_ref, sem):
    pltpu.make_async_copy(src_ref.at[pl.ds(0, dst_ref.shape[0]), :], dst_ref, sem).wait()


def _step_slots():
    i = pl.program_id(0)
    return i, pl.num_programs(0), i % 2


def _gather_norm_body(idx_ref, nxt_ref, src_ref, g_ref, o_ref, buf, sem):
    i, n, slot = _step_slots()

    @pl.when(i == 0)
    def _():
        _start_row_gather(idx_ref, src_ref, buf.at[0], sem.at[0])

    @pl.when(i + 1 < n)
    def _():
        _start_row_gather(nxt_ref, src_ref, buf.at[1 - slot], sem.at[1 - slot])

    _wait_row_gather(src_ref, buf.at[slot], sem.at[slot])
    o_ref[...] = _rms_rows(buf[slot], g_ref[...]).astype(o_ref.dtype)


def gather_norm_rows(src, gain, idx, *, rows):
    n = idx.shape[0]
    d = src.shape[1]
    steps = n // rows
    idx3 = idx.reshape(steps, 1, rows)
    return pl.pallas_call(
        _gather_norm_body,
        grid=(steps,),
        in_specs=[
            pl.BlockSpec((1, 1, rows), lambda i: (i, 0, 0), memory_space=pltpu.SMEM),
            pl.BlockSpec((1, 1, rows), lambda i: (jnp.minimum(i + 1, steps - 1), 0, 0), memory_space=pltpu.SMEM),
            pl.BlockSpec(memory_space=pl.ANY),
            pl.BlockSpec((1, d), lambda i: (0, 0)),
        ],
        out_specs=pl.BlockSpec((rows, d), lambda i: (i, 0)),
        out_shape=jax.ShapeDtypeStruct((n, d), BF16),
        scratch_shapes=[pltpu.VMEM((2, rows, d), src.dtype), pltpu.SemaphoreType.DMA((2,))],
        compiler_params=_cparams("arbitrary"),
        name="gather_norm_rows",
    )(idx3, idx3, src, gain.reshape(1, d))


def _moe_combine_body(p0_ref, p1_ref, n0_ref, n1_ref, y_ref, x_ref, w_ref, o_ref, buf, sem):
    i, n, slot = _step_slots()

    def start(a_ref, b_ref, s):
        _start_row_gather(a_ref, y_ref, buf.at[s, 0], sem.at[s, 0])
        _start_row_gather(b_ref, y_ref, buf.at[s, 1], sem.at[s, 1])

    @pl.when(i == 0)
    def _():
        start(p0_ref, p1_ref, 0)

    @pl.when(i + 1 < n)
    def _():
        start(n0_ref, n1_ref, 1 - slot)

    _wait_row_gather(y_ref, buf.at[slot, 0], sem.at[slot, 0])
    _wait_row_gather(y_ref, buf.at[slot, 1], sem.at[slot, 1])
    w = w_ref[...]
    o_ref[...] = x_ref[...] + w[:, 0:1] * buf[slot, 0] + w[:, 1:2] * buf[slot, 1]


def moe_combine(x, y, pos, wts, *, tm):
    t, d = x.shape
    steps = t // tm
    pos3 = pos.reshape(TOP_K * steps, 1, tm)

    def idx_spec(choice, ahead):
        return pl.BlockSpec((1, 1, tm), lambda i: (choice * steps + jnp.minimum(i + ahead, steps - 1), 0, 0),
                            memory_space=pltpu.SMEM)

    return pl.pallas_call(
        _moe_combine_body,
        grid=(steps,),
        in_specs=[
            idx_spec(0, 0), idx_spec(1, 0), idx_spec(0, 1), idx_spec(1, 1),
            pl.BlockSpec(memory_space=pl.ANY),
            pl.BlockSpec((tm, d), lambda i: (i, 0)),
            pl.BlockSpec((tm, TOP_K), lambda i: (i, 0)),
        ],
        out_specs=pl.BlockSpec((tm, d), lambda i: (i, 0)),
        out_shape=jax.ShapeDtypeStruct((t, d), F32),
        scratch_shapes=[pltpu.VMEM((2, TOP_K, tm, d), y.dtype), pltpu.SemaphoreType.DMA((2, TOP_K))],
        compiler_params=_cparams("arbitrary"),
        name="moe_combine",
    )(pos3, pos3, pos3, pos3, y, x, wts)


def moe_dispatch_plan(top_idx, *, tm):
    k, t = top_idx.shape
    n_rows = k * t + N_EXPERTS * tm
    n_tiles = n_rows // tm
    e = top_idx.reshape(-1)
    onehot = (e[:, None] == jnp.arange(N_EXPERTS, dtype=jnp.int32)[None, :]).astype(jnp.int32)
    csum = jnp.cumsum(onehot, axis=0)
    counts = csum[-1]
    rank = jnp.sum((csum - onehot) * onehot, axis=1)
    padded = ((counts + tm - 1) // tm) * tm
    ends = jnp.cumsum(padded)
    starts = ends - padded
    pos = jnp.sum(onehot * starts[None, :], axis=1) + rank
    token = jnp.tile(jnp.arange(t, dtype=jnp.int32), k)
    row_token = jnp.zeros((n_rows,), jnp.int32).at[pos].set(token)
    n_used = (ends[-1] // tm).astype(jnp.int32)
    tile_start = jnp.minimum(jnp.arange(n_tiles, dtype=jnp.int32), n_used - 1) * tm
    tile_expert = jnp.sum((tile_start[:, None] >= ends[None, :]).astype(jnp.int32), axis=1)
    return row_token, pos.astype(jnp.int32), tile_expert.astype(jnp.int32), n_used.reshape(1)


def moe_ffn(x, gain, w_router, w_gate, w_up, w_down):
    idx, wts = moe_router(x, gain, w_router.T, tm=TM_MOE)
    row_token, pos, tile_expert, n_used = moe_dispatch_plan(idx, tm=TM_MOE)
    hs = gather_norm_rows(x, gain, row_token, rows=GATHER_ROWS)
    a = grouped_swiglu_up(hs, w_gate, w_up, tile_expert, n_used, tm=TM_MOE, tn=TN_FF)
    y = grouped_down(a, w_down, tile_expert, n_used, tm=TM_MOE, tn=TN_WIDE)
    return moe_combine(x, y, pos, wts.T, tm=GATHER_ROWS)


def _rmsnorm_body(x_ref, g_ref, o_ref):
    o_ref[...] = _rms_rows(x_ref[...], g_ref[...])


def rmsnorm(x, gain, *, tm):
    t, k = x.shape
    return pl.pallas_call(
        _rmsnorm_body,
        grid=(t // tm,),
        in_specs=[pl.BlockSpec((tm, k), lambda i: (i, 0)), pl.BlockSpec((1, k), lambda i: (0, 0))],
        out_specs=pl.BlockSpec((tm, k), lambda i: (i, 0)),
        out_shape=jax.ShapeDtypeStruct((t, k), F32),
        compiler_params=_cparams("parallel"),
        name="final_rmsnorm",
    )(x, gain.reshape(1, k))


def _rope_tables(seq, scale):
    half = MLA_ROPE // 2
    inv = ROPE_THETA ** (-jnp.arange(half, dtype=F32) / half)
    ang = jnp.arange(seq).astype(F32)[:, None] * inv[None, :]
    cos, sin = jnp.cos(ang) * scale, jnp.sin(ang) * scale
    z = jnp.zeros_like(cos)
    cat = lambda *p: jnp.concatenate(p, axis=1)
    return cat(cos, cos, z, z), cat(-sin, z, z, z), cat(z, sin, z, z)


def _layout_w_in(w, d_model):
    off_kva = MLA_Q_RANK
    off_dil = off_kva + MLA_KV_RANK + MLA_ROPE
    col_dil = 3 * DIL_HEADS * DIL_DH
    off_gate = off_dil + col_dil
    pad = jnp.zeros((w.shape[0], 2 * LANE - MLA_ROPE), w.dtype)
    small = jnp.concatenate([w[:, :off_dil], pad, w[:, off_gate:]], axis=1)
    return small.astype(BF16), w[:, off_dil:off_gate].astype(BF16)


def _layout_w_uq(w):
    r = w.shape[0]
    w = w.reshape(r, MLA_HEADS, MLA_NOPE + MLA_ROPE)
    w = jnp.pad(w, ((0, 0), (0, 0), (0, MLA_QK - MLA_NOPE - MLA_ROPE)))
    return w.reshape(r, MLA_HEADS * MLA_QK).astype(BF16)


def _layout_w_ukv(w):
    r = w.shape[0]
    w = w.reshape(r, MLA_HEADS, MLA_NOPE + MLA_V)
    k = w[:, :, :MLA_NOPE].reshape(r, MLA_HEADS * MLA_NOPE)
    v = w[:, :, MLA_NOPE:].reshape(r, MLA_HEADS * MLA_V)
    return jnp.concatenate([k, v], axis=1).astype(BF16)


def kernel(x, mix_norm, w_in, q_norm, w_uq, kv_norm, w_ukv, w_o_mla, w_o_dil, w_out, rel_bias,
           ffn_norm, w_ffn_gate, w_ffn_up, w_ffn_down, w_router, w_exp_gate, w_exp_up, w_exp_down,
           final_norm):
    batch, seq, d_model = x.shape
    depth = w_in.shape[0]
    t = batch * seq
    x = x.reshape(t, d_model)

    q_scale = (MLA_NOPE + MLA_ROPE) ** -0.5 * math.log2(math.e)
    q_tabs = _rope_tables(seq, q_scale)
    k_tabs = _rope_tables(seq, 1.0)
    bias_tabs = dilated_bias_tables(rel_bias)
    gate_col = MLA_Q_RANK + MLA_KV_RANK + 2 * LANE

    for layer in range(depth):
        w_small, w_dil = _layout_w_in(w_in[layer], d_model)
        z = norm_matmul(x, mix_norm[layer], w_small, tm=TM, tn=TN_WIDE, sig_from=gate_col // TN_WIDE)
        qkvs = dilated_qkv_proj(x, mix_norm[layer], w_dil, batch=batch, seq=seq, tm=TM, tn=TN)

        q = q_proj(z, q_norm[layer], _layout_w_uq(w_uq[layer]), q_tabs, seq=seq, tm=TQ, scale=q_scale)
        k, v = kv_proj(z, kv_norm[layer], _layout_w_ukv(w_ukv[layer]), k_tabs, seq=seq, tm=TQ)
        o_mla = mla_attention(q, k, v, batch=batch, seq=seq, tq=TQ_ATT)

        os_, lses = [], []
        for g in range(DIL_GROUPS):
            o_g, lse_g = dilated_group_attention(qkvs[g], bias_tabs[g], group=g)
            os_.append(o_g)
            lses.append(lse_g)
        o_dil = dilated_combine(os_, lses, seq=seq, tm=TM_COMBINE)

        merged = merge_branches(o_mla, o_dil, w_o_mla[layer].astype(BF16), w_o_dil[layer].astype(BF16),
                                z, gate_col=gate_col, tm=TM, tn=TN_WIDE)
        x = matmul_residual(merged, w_out[layer].astype(BF16), x, tm=TM_DOWN, tn=d_model)

        i = layer // 2
        if layer % 2 == 0:
            a = swiglu_up(x, ffn_norm[layer], w_ffn_gate[i].astype(BF16), w_ffn_up[i].astype(BF16),
                          tm=TM, tn=TN)
            x = matmul_residual(a, w_ffn_down[i].astype(BF16), x, tm=TM_DOWN, tn=TN_WIDE)
        else:
            x = moe_ffn(x, ffn_norm[layer], w_router[i], w_exp_gate[i].astype(BF16),
                        w_exp_up[i].astype(BF16), w_exp_down[i].astype(BF16))

    return rmsnorm(x, final_norm, tm=TM).reshape(batch, seq, d_model)
```

```python
import functools
import math

import numpy as np
import jax
import jax.numpy as jnp
from jax import lax
from jax.experimental import pallas as pl
from jax.experimental.pallas import tpu as pltpu

F32 = jnp.float32
BF16 = jnp.bfloat16

EPS = 1e-6
NEG = -1e30

MLA_HEADS = 16
MLA_Q_RANK = 512
MLA_KV_RANK = 256
MLA_NOPE = 128
MLA_ROPE = 64
MLA_V = 128
ROPE_THETA = 10000.0
DIL_PAIRS = ((128, 1), (512, 4), (2048, 16))
DIL_GROUPS = 3
DIL_HPG = 8
DIL_HEADS = DIL_GROUPS * DIL_HPG
DIL_DH = 128
DIL_BLOCK = 128
REL_BUCKETS = 32
REL_MAX_DIST = 2048
N_EXPERTS = 8
TOP_K = 2

LANE = 128
MLA_QK = 256
VMEM_LIMIT = 56 * 2**20

TM = 1024
TM_DOWN = 512
TN = 512
TN_WIDE = 1024
TN_FF = 1408
TM_MOE = 512
TQ = 512
TQ_ATT = 2048
MLA_ROW_RUN = 256
DIL_ROW_CHUNK = 64
TM_COMBINE = 512
GATHER_ROWS = 256
NORM_CHUNK = 256


def _cparams(*sem):
    return pltpu.CompilerParams(dimension_semantics=sem, vmem_limit_bytes=VMEM_LIMIT)


def _rms_rows(x, g):
    ms = jnp.mean(x * x, axis=-1, keepdims=True)
    return x * lax.rsqrt(ms + EPS) * g


def _norm_into(x_ref, g_ref, h_scr):
    rows = x_ref.shape[0]
    chunk = min(NORM_CHUNK, rows)
    for r in range(0, rows, chunk):
        x = x_ref[r:r + chunk, :].astype(F32)
        h_scr[r:r + chunk, :] = _rms_rows(x, g_ref[...]).astype(h_scr.dtype)


def _sigmoid(x):
    return 1.0 / (1.0 + jnp.exp(-x))


def _prenorm_body(x_ref, g_ref, o_ref, a_scr, b_scr):
    tm, k = x_ref.shape
    half = a_scr.shape[1]
    chunk = min(NORM_CHUNK, half)
    q4, q16 = half // 4, half // 16
    for s in range(tm // half):
        for r0 in range(0, half, chunk):
            rows = slice(s * half + r0, s * half + r0 + chunk)
            h = _rms_rows(x_ref[rows, :], g_ref[...])
            o_ref[0, rows, :] = h.astype(o_ref.dtype)
            for c in range(k // LANE):
                a_scr[c, r0:r0 + chunk, :] = h[:, c * LANE:(c + 1) * LANE]
        for c in range(k // LANE):
            lanes = slice(c * LANE, (c + 1) * LANE)
            for r in range(4):
                h4 = a_scr[c, pl.ds(r, q4, stride=4), :]
                b_scr[c, r * q4:(r + 1) * q4, :] = h4
                dst = r * (tm // 4) + s * q4
                o_ref[1, dst:dst + q4, lanes] = h4.astype(o_ref.dtype)
            for r in range(4):
                for r2 in range(4):
                    h16 = b_scr[c, pl.ds(r * q4 + r2, q16, stride=4), :]
                    dst = (r + 4 * r2) * (tm // 16) + s * q16
                    o_ref[2, dst:dst + q16, lanes] = h16.astype(o_ref.dtype)


def prenorm_classes(x, gain, *, tm):
    assert tuple(d for _, d in DIL_PAIRS) == (1, 4, 16)
    t, k = x.shape
    half = tm // 2
    return pl.pallas_call(
        _prenorm_body,
        grid=(t // tm,),
        in_specs=[pl.BlockSpec((tm, k), lambda i: (i, 0)), pl.BlockSpec((1, k), lambda i: (0, 0))],
        out_specs=pl.BlockSpec((DIL_GROUPS, tm, k), lambda i: (0, i, 0)),
        out_shape=jax.ShapeDtypeStruct((DIL_GROUPS, t, k), BF16),
        scratch_shapes=[pltpu.VMEM((k // LANE, half, LANE), F32), pltpu.VMEM((k // LANE, half, LANE), F32)],
        compiler_params=_cparams("parallel"),
        name="prenorm_classes",
    )(x, gain.reshape(1, k))


def _mm_sig_body(h_ref, w_ref, o_ref, *, sig_from):
    j = pl.program_id(1)
    acc = jnp.dot(h_ref[...], w_ref[...], preferred_element_type=F32)

    @pl.when(j < sig_from)
    def _():
        o_ref[...] = acc.astype(o_ref.dtype)

    @pl.when(j >= sig_from)
    def _():
        o_ref[...] = _sigmoid(acc).astype(o_ref.dtype)


def matmul_sigmoid_tail(h3, w, *, tm, tn, sig_from):
    _, t, k = h3.shape
    n = w.shape[1]
    return pl.pallas_call(
        functools.partial(_mm_sig_body, sig_from=sig_from),
        grid=(t // tm, n // tn),
        in_specs=[
            pl.BlockSpec((None, tm, k), lambda i, j: (0, i, 0)),
            pl.BlockSpec((k, tn), lambda i, j: (0, j)),
        ],
        out_specs=pl.BlockSpec((tm, tn), lambda i, j: (i, j)),
        out_shape=jax.ShapeDtypeStruct((t, n), BF16),
        compiler_params=_cparams("parallel", "arbitrary"),
        name="mixer_in_proj",
    )(h3, w)


def _dil_proj_body(h_ref, w_ref, o0_ref, o1_ref, o2_ref, *, tiles_per_group):
    tm = h_ref.shape[0]
    group = pl.program_id(1) // tiles_per_group
    acc = jnp.dot(h_ref[...], w_ref[...], preferred_element_type=F32)
    for g, o_ref in enumerate((o0_ref, o1_ref, o2_ref)):
        d = o_ref.shape[0]
        cs = tm // d

        @pl.when(group == g)
        def _(o_ref=o_ref, d=d, cs=cs):
            for r in range(d):
                o_ref[r] = acc[r * cs:(r + 1) * cs, :].astype(o_ref.dtype)


def dilated_qkv_proj(h3, w, *, batch, seq, tm, tn):
    _, t, k = h3.shape
    unit = DIL_HPG * DIL_DH
    upt = unit // tn
    tpg = 3 * upt
    per_seq = seq // tm

    def w_col(j):
        return ((j % tpg) // upt) * (DIL_GROUPS * upt) + (j // tpg) * upt + j % upt

    out_specs, out_shape = [], []
    for g, (_, d) in enumerate(DIL_PAIRS):
        out_specs.append(pl.BlockSpec(
            (None, d, tm // d, tn),
            lambda i, j, g=g: (i // per_seq, 0, i % per_seq, jnp.clip(j - g * tpg, 0, tpg - 1))))
        out_shape.append(jax.ShapeDtypeStruct((batch, d, seq // d, 3 * unit), BF16))
    return pl.pallas_call(
        functools.partial(_dil_proj_body, tiles_per_group=tpg),
        grid=(t // tm, DIL_GROUPS * tpg),
        in_specs=[
            pl.BlockSpec((None, tm, k), lambda i, j: (j // tpg, i, 0)),
            pl.BlockSpec((k, tn), lambda i, j: (0, w_col(j))),
        ],
        out_specs=out_specs,
        out_shape=out_shape,
        compiler_params=_cparams("parallel", "arbitrary"),
        name="dilated_qkv_proj",
    )(h3, w)


def _rope128(pe, c_ref, sa_ref, sb_ref):
    return (pe * c_ref[...]
            + pltpu.roll(pe, 96, 1) * sa_ref[...]
            + pltpu.roll(pe, 32, 1) * sb_ref[...])


def _q_proj_body(qa_ref, g_ref, w_ref, c_ref, sa_ref, sb_ref, o_ref, h_scr, *, scale):
    _norm_into(qa_ref, g_ref, h_scr)
    for h in range(MLA_HEADS):
        acc = jnp.dot(h_scr[...], w_ref[:, h * MLA_QK:(h + 1) * MLA_QK], preferred_element_type=F32)
        o_ref[:, h * MLA_QK:h * MLA_QK + LANE] = (acc[:, :LANE] * scale).astype(o_ref.dtype)
        pe = _rope128(acc[:, LANE:], c_ref, sa_ref, sb_ref)
        o_ref[:, h * MLA_QK + LANE:(h + 1) * MLA_QK] = pe.astype(o_ref.dtype)


def q_proj(z, gain, w, tabs, *, seq, tm, scale):
    t = z.shape[0]
    r = w.shape[0]
    n = w.shape[1]
    per_seq = seq // tm
    tab_spec = pl.BlockSpec((tm, LANE), lambda i: (i % per_seq, 0))
    return pl.pallas_call(
        functools.partial(_q_proj_body, scale=scale),
        grid=(t // tm,),
        in_specs=[
            pl.BlockSpec((tm, r), lambda i: (i, 0)),
            pl.BlockSpec((1, r), lambda i: (0, 0)),
            pl.BlockSpec((r, n), lambda i: (0, 0)),
            tab_spec, tab_spec, tab_spec,
        ],
        out_specs=pl.BlockSpec((tm, n), lambda i: (i, 0)),
        out_shape=jax.ShapeDtypeStruct((t, n), BF16),
        scratch_shapes=[pltpu.VMEM((tm, r), BF16)],
        compiler_params=_cparams("parallel"),
        name="mla_q_proj",
    )(z, gain.reshape(1, r), w, *tabs)


def _kv_proj_body(ckv_ref, kpe_ref, g_ref, w_ref, c_ref, sa_ref, sb_ref, k_ref, v_ref, h_scr):
    _norm_into(ckv_ref, g_ref, h_scr)
    pe = _rope128(kpe_ref[...].astype(F32), c_ref, sa_ref, sb_ref).astype(k_ref.dtype)
    nk = MLA_HEADS * MLA_NOPE
    for h in range(MLA_HEADS):
        kn = jnp.dot(h_scr[...], w_ref[:, h * MLA_NOPE:(h + 1) * MLA_NOPE], preferred_element_type=F32)
        k_ref[:, h * MLA_QK:h * MLA_QK + LANE] = kn.astype(k_ref.dtype)
        k_ref[:, h * MLA_QK + LANE:(h + 1) * MLA_QK] = pe
    v = jnp.dot(h_scr[...], w_ref[:, nk:], preferred_element_type=F32)
    v_ref[...] = v.astype(v_ref.dtype)


def kv_proj(z, gain, w, tabs, *, seq, tm):
    t = z.shape[0]
    r = w.shape[0]
    per_seq = seq // tm
    nv = MLA_HEADS * MLA_V
    tab_spec = pl.BlockSpec((tm, LANE), lambda i: (i % per_seq, 0))
    return pl.pallas_call(
        _kv_proj_body,
        grid=(t // tm,),
        in_specs=[
            pl.BlockSpec((tm, r), lambda i: (i, MLA_Q_RANK // r)),
            pl.BlockSpec((tm, LANE), lambda i: (i, (MLA_Q_RANK + r) // LANE)),
            pl.BlockSpec((1, r), lambda i: (0, 0)),
            pl.BlockSpec(w.shape, lambda i: (0, 0)),
            tab_spec, tab_spec, tab_spec,
        ],
        out_specs=[
            pl.BlockSpec((tm, MLA_HEADS * MLA_QK), lambda i: (i, 0)),
            pl.BlockSpec((tm, nv), lambda i: (i, 0)),
        ],
        out_shape=[
            jax.ShapeDtypeStruct((t, MLA_HEADS * MLA_QK), BF16),
            jax.ShapeDtypeStruct((t, nv), BF16),
        ],
        scratch_shapes=[pltpu.VMEM((tm, r), BF16)],
        compiler_params=_cparams("parallel"),
        name="mla_kv_proj",
    )(z, z, gain.reshape(1, r), w, *tabs)


def _dot_nt(a, b):
    return lax.dot_general(a, b, (((1,), (1,)), ((), ())), preferred_element_type=F32)


def _mla_attn_body(q_ref, k_ref, v_ref, o_ref, *, tq, tr):
    qi = pl.program_id(2)
    row = lax.broadcasted_iota(jnp.int32, (tr, tr), 0)
    col = lax.broadcasted_iota(jnp.int32, (tr, tr), 1)

    def run(first):
        rows = slice(first % tq, first % tq + tr)
        q = q_ref[rows, :]
        s_d = jnp.where(col <= row, _dot_nt(q, k_ref[first:first + tr, :]), NEG)
        m = jnp.max(s_d, axis=-1, keepdims=True)
        if first:
            s_o = _dot_nt(q, k_ref[:first, :])
            m = jnp.maximum(m, jnp.max(s_o, axis=-1, keepdims=True))
        p_d = jnp.exp2(s_d - m)
        l = jnp.sum(p_d, axis=-1, keepdims=True)
        o = jnp.dot(p_d.astype(v_ref.dtype), v_ref[first:first + tr, :], preferred_element_type=F32)
        if first:
            p_o = jnp.exp2(s_o - m)
            l = l + jnp.sum(p_o, axis=-1, keepdims=True)
            o = o + jnp.dot(p_o.astype(v_ref.dtype), v_ref[:first, :], preferred_element_type=F32)
        o_ref[rows, :] = (o / l).astype(o_ref.dtype)

    for n in range(k_ref.shape[0] // tq):
        @pl.when(qi == n)
        def _(n=n):
            for first in range(n * tq, (n + 1) * tq, tr):
                run(first)


def mla_attention(q, k, v, *, batch, seq, tq):
    t = batch * seq
    q3 = q.reshape(batch, seq, MLA_HEADS * MLA_QK)
    k3 = k.reshape(batch, seq, MLA_HEADS * MLA_QK)
    v3 = v.reshape(batch, seq, MLA_HEADS * MLA_V)
    out = pl.pallas_call(
        functools.partial(_mla_attn_body, tq=tq, tr=MLA_ROW_RUN),
        grid=(batch, MLA_HEADS, seq // tq),
        in_specs=[
            pl.BlockSpec((None, tq, MLA_QK), lambda b, h, i: (b, i, h)),
            pl.BlockSpec((None, seq, MLA_QK), lambda b, h, i: (b, 0, h)),
            pl.BlockSpec((None, seq, MLA_V), lambda b, h, i: (b, 0, h)),
        ],
        out_specs=pl.BlockSpec((None, tq, MLA_V), lambda b, h, i: (b, i, h)),
        out_shape=jax.ShapeDtypeStruct((batch, seq, MLA_HEADS * MLA_V), BF16),
        compiler_params=_cparams("parallel", "parallel", "arbitrary"),
        name="mla_attention",
    )(q3, k3, v3)
    return out.reshape(t, MLA_HEADS * MLA_V)


def _dil_attn_body(q_ref, kp_ref, kc_ref, vp_ref, vc_ref, b_ref, o_ref, lse_ref, s_scr, p_scr, m_scr,
                   *, scale, rc):
    blk = q_ref.shape[0]
    for h in range(DIL_HPG):
        sl = slice(h * DIL_DH, (h + 1) * DIL_DH)
        rows = slice(h * blk, (h + 1) * blk)
        q = q_ref[:, sl]
        s_scr[rows, :blk] = _dot_nt(q, kp_ref[:, sl]) * scale + b_ref[h, :, :blk]
        s_scr[rows, blk:] = _dot_nt(q, kc_ref[:, sl]) * scale + b_ref[h, :, blk:]
    for c in range(DIL_HPG * blk // rc):
        rows = slice(c * rc, (c + 1) * rc)
        s = s_scr[rows, :]
        m = jnp.max(s, axis=-1, keepdims=True)
        p_scr[rows, :] = jnp.exp(s - m).astype(p_scr.dtype)
        m_scr[rows, :] = jnp.broadcast_to(m, (rc, LANE))
    ones = jnp.ones((2 * blk, LANE), p_scr.dtype)
    lane = lax.broadcasted_iota(jnp.int32, (blk, LANE), 1)
    lse_tile = jnp.zeros((blk, LANE), F32)
    for h in range(DIL_HPG):
        sl = slice(h * DIL_DH, (h + 1) * DIL_DH)
        rows = slice(h * blk, (h + 1) * blk)
        o = (jnp.dot(p_scr[rows, :blk], vp_ref[:, sl], preferred_element_type=F32)
             + jnp.dot(p_scr[rows, blk:], vc_ref[:, sl], preferred_element_type=F32))
        l = jnp.dot(p_scr[rows, :], ones, preferred_element_type=F32)
        o_ref[:, sl] = (o / l).astype(o_ref.dtype)
        lse_tile = jnp.where(lane == h, m_scr[rows, :] + jnp.log(l), lse_tile)
    lse_ref[...] = lse_tile


def dilated_group_attention(qkv, bias, *, group):
    batch, dilation, ln, _ = qkv.shape
    hw = DIL_HPG * DIL_DH
    nb = ln // DIL_BLOCK

    def spec(col, prev):
        if prev:
            return pl.BlockSpec((None, None, DIL_BLOCK, hw), lambda b, r, n: (b, r, jnp.maximum(n - 1, 0), col))
        return pl.BlockSpec((None, None, DIL_BLOCK, hw), lambda b, r, n: (b, r, n, col))

    return pl.pallas_call(
        functools.partial(_dil_attn_body, scale=DIL_DH ** -0.5, rc=DIL_ROW_CHUNK),
        grid=(batch, dilation, nb),
        in_specs=[
            spec(0, False), spec(1, True), spec(1, False), spec(2, True), spec(2, False),
            pl.BlockSpec((None, DIL_HPG, DIL_BLOCK, 2 * DIL_BLOCK), lambda b, r, n: (jnp.minimum(n, 1), 0, 0, 0)),
        ],
        out_specs=[
            pl.BlockSpec((None, None, DIL_BLOCK, hw), lambda b, r, n: (b, r, n, 0)),
            pl.BlockSpec((None, None, DIL_BLOCK, LANE), lambda b, r, n: (b, r, n, 0)),
        ],
        out_shape=[
            jax.ShapeDtypeStruct((batch, dilation, ln, hw), BF16),
            jax.ShapeDtypeStruct((batch, dilation, ln, LANE), F32),
        ],
        scratch_shapes=[
            pltpu.VMEM((DIL_HPG * DIL_BLOCK, 2 * DIL_BLOCK), F32),
            pltpu.VMEM((DIL_HPG * DIL_BLOCK, 2 * DIL_BLOCK), BF16),
            pltpu.VMEM((DIL_HPG * DIL_BLOCK, LANE), F32),
        ],
        compiler_params=_cparams("parallel", "parallel", "arbitrary"),
        name=f"dilated_attention_g{group}",
    )(qkv, qkv, qkv, qkv, qkv, bias)


def _to_token_order(src_ref, dst_ref):
    d, cs, w = src_ref.shape
    for r in range(d):
        for c in range(w // LANE):
            dst_ref[c, pl.ds(r, cs, stride=d), :] = src_ref[r, :, c * LANE:(c + 1) * LANE].astype(dst_ref.dtype)


def _dil_combine_body(o0_ref, o1_ref, o2_ref, l0_ref, l1_ref, l2_ref, out_ref, o_scr, l_scr):
    for g, (o_ref, l_ref) in enumerate(((o1_ref, l1_ref), (o2_ref, l2_ref))):
        _to_token_order(o_ref, o_scr.at[g])
        _to_token_order(l_ref, l_scr.at[g])
    for h in range(DIL_HPG):
        sl = slice(h * DIL_DH, (h + 1) * DIL_DH)
        ls = [l0_ref[0, :, h:h + 1], l_scr[0, 0, :, h:h + 1], l_scr[1, 0, :, h:h + 1]]
        os_ = [o0_ref[0, :, sl].astype(F32), o_scr[0, h], o_scr[1, h]]
        m = jnp.maximum(jnp.maximum(ls[0], ls[1]), ls[2])
        es = [jnp.exp(x - m) for x in ls]
        den = es[0] + es[1] + es[2]
        acc = sum((e / den) * o for e, o in zip(es, os_))
        out_ref[:, sl] = acc.astype(out_ref.dtype)


def dilated_combine(os_, lses, *, seq, tm):
    batch = os_[0].shape[0]
    hw = os_[0].shape[-1]
    t = batch * seq
    per_seq = seq // tm

    def spec(arr):
        d, w = arr.shape[1], arr.shape[3]
        return pl.BlockSpec((None, d, tm // d, w), lambda i: (i // per_seq, 0, i % per_seq, 0))

    return pl.pallas_call(
        _dil_combine_body,
        grid=(t // tm,),
        in_specs=[spec(a) for a in (*os_, *lses)],
        out_specs=pl.BlockSpec((tm, hw), lambda i: (i, 0)),
        out_shape=jax.ShapeDtypeStruct((t, hw), BF16),
        scratch_shapes=[
            pltpu.VMEM((DIL_GROUPS - 1, hw // LANE, tm, LANE), F32),
            pltpu.VMEM((DIL_GROUPS - 1, 1, tm, LANE), F32),
        ],
        compiler_params=_cparams("parallel"),
        name="dilated_combine",
    )(*os_, *lses)


def _t5_bucket_np(dist):
    max_exact = REL_BUCKETS // 2
    large = max_exact + (np.log(np.maximum(dist, 1).astype(np.float32) / max_exact)
                         / math.log(REL_MAX_DIST / max_exact) * (REL_BUCKETS - max_exact)).astype(np.int32)
    large = np.minimum(large, REL_BUCKETS - 1)
    return np.where(dist < max_exact, dist, large)


def dilated_bias_tables(rel_bias):
    iq = np.arange(DIL_BLOCK)[:, None]
    jk = np.arange(2 * DIL_BLOCK)[None, :]
    dist_c = DIL_BLOCK + iq - jk
    tables = []
    for g, (window, dilation) in enumerate(DIL_PAIRS):
        span = window // dilation
        band = (dist_c >= 0) & (dist_c <= span)
        bucket = _t5_bucket_np(np.clip(dist_c, 0, None) * dilation)
        onehot = (bucket.reshape(-1, 1) == np.arange(REL_BUCKETS)[None, :]).astype(np.float32)
        heads = rel_bias[:, g * DIL_HPG:(g + 1) * DIL_HPG].astype(F32)
        bias = jnp.dot(onehot, heads, precision=lax.Precision.HIGHEST)
        bias = bias.reshape(DIL_BLOCK, 2 * DIL_BLOCK, DIL_HPG).transpose(2, 0, 1)
        general = jnp.where(band[None], bias, NEG)
        first = jnp.where((band & (jk >= DIL_BLOCK))[None], bias, NEG)
        tables.append(jnp.stack([first, general]))
    return tables


def _merge_body(a_ref, b_ref, wa_ref, wb_ref, ga_ref, gb_ref, o_ref):
    ya = jnp.dot(a_ref[...], wa_ref[...], preferred_element_type=F32)
    yb = jnp.dot(b_ref[...], wb_ref[...], preferred_element_type=F32)
    o_ref[...] = (ga_ref[...].astype(F32) * ya + gb_ref[...].astype(F32) * yb).astype(o_ref.dtype)


def merge_branches(o_a, o_b, w_a, w_b, z, *, gate_col, tm, tn):
    t, ka = o_a.shape
    kb = o_b.shape[1]
    n = w_a.shape[1]
    g0 = gate_col // tn
    return pl.pallas_call(
        _merge_body,
        grid=(t // tm, n // tn),
        in_specs=[
            pl.BlockSpec((tm, ka), lambda i, j: (i, 0)),
            pl.BlockSpec((tm, kb), lambda i, j: (i, 0)),
            pl.BlockSpec((ka, tn), lambda i, j: (0, j)),
            pl.BlockSpec((kb, tn), lambda i, j: (0, j)),
            pl.BlockSpec((tm, tn), lambda i, j: (i, g0 + j)),
            pl.BlockSpec((tm, tn), lambda i, j: (i, g0 + n // tn + j)),
        ],
        out_specs=pl.BlockSpec((tm, tn), lambda i, j: (i, j)),
        out_shape=jax.ShapeDtypeStruct((t, n), BF16),
        compiler_params=_cparams("parallel", "arbitrary"),
        name="merge_branches",
    )(o_a, o_b, w_a, w_b, z, z)


def _mm_residual_body(a_ref, w_ref, x_ref, o_ref):
    o_ref[...] = x_ref[...] + jnp.dot(a_ref[...], w_ref[...], preferred_element_type=F32)


def matmul_residual(a, w, x, *, tm, tn):
    t, k = a.shape
    n = w.shape[1]
    return pl.pallas_call(
        _mm_residual_body,
        grid=(t // tm, n // tn),
        in_specs=[
            pl.BlockSpec((tm, k), lambda i, j: (i, 0)),
            pl.BlockSpec((k, tn), lambda i, j: (0, j)),
            pl.BlockSpec((tm, tn), lambda i, j: (i, j)),
        ],
        out_specs=pl.BlockSpec((tm, tn), lambda i, j: (i, j)),
        out_shape=jax.ShapeDtypeStruct((t, n), F32),
        compiler_params=_cparams("parallel", "arbitrary"),
        name="matmul_residual",
    )(a, w, x)


def _swiglu_up(h, wg, wu):
    g = jnp.dot(h, wg, preferred_element_type=F32)
    u = jnp.dot(h, wu, preferred_element_type=F32)
    return g * _sigmoid(g) * u


def _swiglu_up_body(x_ref, g_ref, wg_ref, wu_ref, o_ref, h_scr):
    @pl.when(pl.program_id(1) == 0)
    def _():
        _norm_into(x_ref, g_ref, h_scr)

    o_ref[...] = _swiglu_up(h_scr[...], wg_ref[...], wu_ref[...]).astype(o_ref.dtype)


def swiglu_up(x, gain, w_gate, w_up, *, tm, tn):
    t, k = x.shape
    n = w_gate.shape[1]
    w_spec = pl.BlockSpec((k, tn), lambda i, j: (0, j))
    return pl.pallas_call(
        _swiglu_up_body,
        grid=(t // tm, n // tn),
        in_specs=[
            pl.BlockSpec((tm, k), lambda i, j: (i, 0)),
            pl.BlockSpec((1, k), lambda i, j: (0, 0)),
            w_spec, w_spec,
        ],
        out_specs=pl.BlockSpec((tm, tn), lambda i, j: (i, j)),
        out_shape=jax.ShapeDtypeStruct((t, n), BF16),
        scratch_shapes=[pltpu.VMEM((tm, k), BF16)],
        compiler_params=_cparams("parallel", "arbitrary"),
        name="swiglu_up",
    )(x, gain.reshape(1, k), w_gate, w_up)


def _grouped_up_body(te_ref, nu_ref, h_ref, wg_ref, wu_ref, o_ref):
    i = pl.program_id(0)

    @pl.when(i < nu_ref[0])
    def _():
        o_ref[...] = _swiglu_up(h_ref[...], wg_ref[...], wu_ref[...]).astype(o_ref.dtype)

    @pl.when(i >= nu_ref[0])
    def _():
        o_ref[...] = jnp.zeros_like(o_ref)


def grouped_swiglu_up(hs, w_gate, w_up, tile_expert, n_used, *, tm, tn):
    r, k = hs.shape
    n = w_gate.shape[2]
    nj = n // tn

    def row(i, nu):
        return jnp.minimum(i, nu[0] - 1)

    def col(i, j, nu):
        return jnp.where(i < nu[0], j, nj - 1)

    w_spec = pl.BlockSpec((None, k, tn), lambda i, j, te, nu: (te[i], 0, col(i, j, nu)))
    return pl.pallas_call(
        _grouped_up_body,
        grid_spec=pltpu.PrefetchScalarGridSpec(
            num_scalar_prefetch=2,
            grid=(r // tm, nj),
            in_specs=[
                pl.BlockSpec((tm, k), lambda i, j, te, nu: (row(i, nu), 0)),
                w_spec, w_spec,
            ],
            out_specs=pl.BlockSpec((tm, tn), lambda i, j, te, nu: (i, j)),
        ),
        out_shape=jax.ShapeDtypeStruct((r, n), BF16),
        compiler_params=_cparams("arbitrary", "arbitrary"),
        name="grouped_swiglu_up",
    )(tile_expert, n_used, hs, w_gate, w_up)


def _grouped_down_body(te_ref, nu_ref, a_ref, w_ref, o_ref):
    i = pl.program_id(0)

    @pl.when(i < nu_ref[0])
    def _():
        o_ref[...] = jnp.dot(a_ref[...], w_ref[...], preferred_element_type=F32)

    @pl.when(i >= nu_ref[0])
    def _():
        o_ref[...] = jnp.zeros_like(o_ref)


def grouped_down(a, w_down, tile_expert, n_used, *, tm, tn):
    r, k = a.shape
    n = w_down.shape[2]
    nj = n // tn

    def row(i, nu):
        return jnp.minimum(i, nu[0] - 1)

    def col(i, j, nu):
        return jnp.where(i < nu[0], j, nj - 1)

    return pl.pallas_call(
        _grouped_down_body,
        grid_spec=pltpu.PrefetchScalarGridSpec(
            num_scalar_prefetch=2,
            grid=(r // tm, nj),
            in_specs=[
                pl.BlockSpec((tm, k), lambda i, j, te, nu: (row(i, nu), 0)),
                pl.BlockSpec((None, k, tn), lambda i, j, te, nu: (te[i], 0, col(i, j, nu))),
            ],
            out_specs=pl.BlockSpec((tm, tn), lambda i, j, te, nu: (i, j)),
        ),
        out_shape=jax.ShapeDtypeStruct((r, n), F32),
        compiler_params=_cparams("arbitrary", "arbitrary"),
        name="grouped_down",
    )(tile_expert, n_used, a, w_down)


def _router_body(x_ref, g_ref, wr_ref, idx_ref, wt_ref):
    h = _rms_rows(x_ref[...], g_ref[...])
    logits = lax.dot_general(wr_ref[...], h, (((1,), (1,)), ((), ())),
                             precision=lax.Precision.HIGHEST, preferred_element_type=F32)
    e = lax.broadcasted_iota(jnp.int32, logits.shape, 0)
    n_e = logits.shape[0]
    m1 = jnp.max(logits, axis=0, keepdims=True)
    i1 = jnp.min(jnp.where(logits == m1, e, n_e), axis=0, keepdims=True)
    rest = jnp.where(e == i1, -jnp.inf, logits)
    m2 = jnp.max(rest, axis=0, keepdims=True)
    i2 = jnp.min(jnp.where(rest == m2, e, n_e), axis=0, keepdims=True)
    t = jnp.exp(m2 - m1)
    idx_ref[0:1, :] = i1
    idx_ref[1:2, :] = i2
    wt_ref[0:1, :] = 1.0 / (1.0 + t)
    wt_ref[1:2, :] = t / (1.0 + t)


def moe_router(x, gain, w_router_t, *, tm):
    t, k = x.shape
    n_e = w_router_t.shape[0]
    out_spec = pl.BlockSpec((TOP_K, tm), lambda i: (0, i))
    return pl.pallas_call(
        _router_body,
        grid=(t // tm,),
        in_specs=[
            pl.BlockSpec((tm, k), lambda i: (i, 0)),
            pl.BlockSpec((1, k), lambda i: (0, 0)),
            pl.BlockSpec((n_e, k), lambda i: (0, 0)),
        ],
        out_specs=[out_spec, out_spec],
        out_shape=[jax.ShapeDtypeStruct((TOP_K, t), jnp.int32), jax.ShapeDtypeStruct((TOP_K, t), F32)],
        compiler_params=_cparams("parallel"),
        name="moe_router",
    )(x, gain.reshape(1, k), w_router_t)


def _start_row_gather(idx_ref, src_ref, dst_ref, sem):
    for r in range(dst_ref.shape[0]):
        pltpu.make_async_copy(src_ref.at[pl.ds(idx_ref[0, 0, r], 1), :], dst_ref.at[pl.ds(r, 1), :], sem).start()


def _wait_row_gather(src_ref, dst_ref, sem):
    pltpu.make_async_copy(src_ref.at[pl.ds(0, dst_ref.shape[0]), :], dst_ref, sem).wait()


def _step_slots():
    i = pl.program_id(0)
    return i, pl.num_programs(0), i % 2


def _gather_norm_body(idx_ref, nxt_ref, src_ref, g_ref, o_ref, buf, sem):
    i, n, slot = _step_slots()

    @pl.when(i == 0)
    def _():
        _start_row_gather(idx_ref, src_ref, buf.at[0], sem.at[0])

    @pl.when(i + 1 < n)
    def _():
        _start_row_gather(nxt_ref, src_ref, buf.at[1 - slot], sem.at[1 - slot])

    _wait_row_gather(src_ref, buf.at[slot], sem.at[slot])
    o_ref[...] = _rms_rows(buf[slot], g_ref[...]).astype(o_ref.dtype)


def gather_norm_rows(src, gain, idx, *, rows):
    n = idx.shape[0]
    d = src.shape[1]
    steps = n // rows
    idx3 = idx.reshape(steps, 1, rows)
    return pl.pallas_call(
        _gather_norm_body,
        grid=(steps,),
        in_specs=[
            pl.BlockSpec((1, 1, rows), lambda i: (i, 0, 0), memory_space=pltpu.SMEM),
            pl.BlockSpec((1, 1, rows), lambda i: (jnp.minimum(i + 1, steps - 1), 0, 0), memory_space=pltpu.SMEM),
            pl.BlockSpec(memory_space=pl.ANY),
            pl.BlockSpec((1, d), lambda i: (0, 0)),
        ],
        out_specs=pl.BlockSpec((rows, d), lambda i: (i, 0)),
        out_shape=jax.ShapeDtypeStruct((n, d), BF16),
        scratch_shapes=[pltpu.VMEM((2, rows, d), src.dtype), pltpu.SemaphoreType.DMA((2,))],
        compiler_params=_cparams("arbitrary"),
        name="gather_norm_rows",
    )(idx3, idx3, src, gain.reshape(1, d))


def _moe_combine_body(p0_ref, p1_ref, n0_ref, n1_ref, y_ref, x_ref, w_ref, o_ref, buf, sem):
    i, n, slot = _step_slots()

    def start(a_ref, b_ref, s):
        _start_row_gather(a_ref, y_ref, buf.at[s, 0], sem.at[s, 0])
        _start_row_gather(b_ref, y_ref, buf.at[s, 1], sem.at[s, 1])

    @pl.when(i == 0)
    def _():
        start(p0_ref, p1_ref, 0)

    @pl.when(i + 1 < n)
    def _():
        start(n0_ref, n1_ref, 1 - slot)

    _wait_row_gather(y_ref, buf.at[slot, 0], sem.at[slot, 0])
    _wait_row_gather(y_ref, buf.at[slot, 1], sem.at[slot, 1])
    w = w_ref[...]
    o_ref[...] = x_ref[...] + w[:, 0:1] * buf[slot, 0] + w[:, 1:2] * buf[slot, 1]


def moe_combine(x, y, pos, wts, *, tm):
    t, d = x.shape
    steps = t // tm
    pos3 = pos.reshape(TOP_K * steps, 1, tm)

    def idx_spec(choice, ahead):
        return pl.BlockSpec((1, 1, tm), lambda i: (choice * steps + jnp.minimum(i + ahead, steps - 1), 0, 0),
                            memory_space=pltpu.SMEM)

    return pl.pallas_call(
        _moe_combine_body,
        grid=(steps,),
        in_specs=[
            idx_spec(0, 0), idx_spec(1, 0), idx_spec(0, 1), idx_spec(1, 1),
            pl.BlockSpec(memory_space=pl.ANY),
            pl.BlockSpec((tm, d), lambda i: (i, 0)),
            pl.BlockSpec((tm, TOP_K), lambda i: (i, 0)),
        ],
        out_specs=pl.BlockSpec((tm, d), lambda i: (i, 0)),
        out_shape=jax.ShapeDtypeStruct((t, d), F32),
        scratch_shapes=[pltpu.VMEM((2, TOP_K, tm, d), y.dtype), pltpu.SemaphoreType.DMA((2, TOP_K))],
        compiler_params=_cparams("arbitrary"),
        name="moe_combine",
    )(pos3, pos3, pos3, pos3, y, x, wts)


def moe_dispatch_plan(top_idx, *, tm):
    k, t = top_idx.shape
    n_rows = k * t + N_EXPERTS * tm
    n_tiles = n_rows // tm
    e = top_idx.reshape(-1)
    onehot = (e[:, None] == jnp.arange(N_EXPERTS, dtype=jnp.int32)[None, :]).astype(jnp.int32)
    csum = jnp.cumsum(onehot, axis=0)
    counts = csum[-1]
    rank = jnp.sum((csum - onehot) * onehot, axis=1)
    padded = ((counts + tm - 1) // tm) * tm
    ends = jnp.cumsum(padded)
    starts = ends - padded
    pos = jnp.sum(onehot * starts[None, :], axis=1) + rank
    token = jnp.tile(jnp.arange(t, dtype=jnp.int32), k)
    row_token = jnp.zeros((n_rows,), jnp.int32).at[pos].set(token)
    n_used = (ends[-1] // tm).astype(jnp.int32)
    tile_start = jnp.minimum(jnp.arange(n_tiles, dtype=jnp.int32), n_used - 1) * tm
    tile_expert = jnp.sum((tile_start[:, None] >= ends[None, :]).astype(jnp.int32), axis=1)
    return row_token, pos.astype(jnp.int32), tile_expert.astype(jnp.int32), n_used.reshape(1)


def moe_ffn(x, gain, w_router, w_gate, w_up, w_down):
    idx, wts = moe_router(x, gain, w_router.T, tm=TM_MOE)
    row_token, pos, tile_expert, n_used = moe_dispatch_plan(idx, tm=TM_MOE)
    hs = gather_norm_rows(x, gain, row_token, rows=GATHER_ROWS)
    a = grouped_swiglu_up(hs, w_gate, w_up, tile_expert, n_used, tm=TM_MOE, tn=TN_FF)
    y = grouped_down(a, w_down, tile_expert, n_used, tm=TM_MOE, tn=TN_WIDE)
    return moe_combine(x, y, pos, wts.T, tm=GATHER_ROWS)


def _rmsnorm_body(x_ref, g_ref, o_ref):
    o_ref[...] = _rms_rows(x_ref[...], g_ref[...])


def rmsnorm(x, gain, *, tm):
    t, k = x.shape
    return pl.pallas_call(
        _rmsnorm_body,
        grid=(t // tm,),
        in_specs=[pl.BlockSpec((tm, k), lambda i: (i, 0)), pl.BlockSpec((1, k), lambda i: (0, 0))],
        out_specs=pl.BlockSpec((tm, k), lambda i: (i, 0)),
        out_shape=jax.ShapeDtypeStruct((t, k), F32),
        compiler_params=_cparams("parallel"),
        name="final_rmsnorm",
    )(x, gain.reshape(1, k))


def _rope_tables(seq, scale):
    half = MLA_ROPE // 2
    inv = ROPE_THETA ** (-jnp.arange(half, dtype=F32) / half)
    ang = jnp.arange(seq).astype(F32)[:, None] * inv[None, :]
    cos, sin = jnp.cos(ang) * scale, jnp.sin(ang) * scale
    z = jnp.zeros_like(cos)
    cat = lambda *p: jnp.concatenate(p, axis=1)
    return cat(cos, cos, z, z), cat(-sin, z, z, z), cat(z, sin, z, z)


def _layout_w_in(w, d_model):
    off_kva = MLA_Q_RANK
    off_dil = off_kva + MLA_KV_RANK + MLA_ROPE
    col_dil = 3 * DIL_HEADS * DIL_DH
    off_gate = off_dil + col_dil
    pad = jnp.zeros((w.shape[0], 2 * LANE - MLA_ROPE), w.dtype)
    small = jnp.concatenate([w[:, :off_dil], pad, w[:, off_gate:]], axis=1)
    return small.astype(BF16), w[:, off_dil:off_gate].astype(BF16)


def _layout_w_uq(w):
    r = w.shape[0]
    w = w.reshape(r, MLA_HEADS, MLA_NOPE + MLA_ROPE)
    w = jnp.pad(w, ((0, 0), (0, 0), (0, MLA_QK - MLA_NOPE - MLA_ROPE)))
    return w.reshape(r, MLA_HEADS * MLA_QK).astype(BF16)


def _layout_w_ukv(w):
    r = w.shape[0]
    w = w.reshape(r, MLA_HEADS, MLA_NOPE + MLA_V)
    k = w[:, :, :MLA_NOPE].reshape(r, MLA_HEADS * MLA_NOPE)
    v = w[:, :, MLA_NOPE:].reshape(r, MLA_HEADS * MLA_V)
    return jnp.concatenate([k, v], axis=1).astype(BF16)


def kernel(x, mix_norm, w_in, q_norm, w_uq, kv_norm, w_ukv, w_o_mla, w_o_dil, w_out, rel_bias,
           ffn_norm, w_ffn_gate, w_ffn_up, w_ffn_down, w_router, w_exp_gate, w_exp_up, w_exp_down,
           final_norm):
    batch, seq, d_model = x.shape
    depth = w_in.shape[0]
    t = batch * seq
    x = x.reshape(t, d_model)

    q_scale = (MLA_NOPE + MLA_ROPE) ** -0.5 * math.log2(math.e)
    q_tabs = _rope_tables(seq, q_scale)
    k_tabs = _rope_tables(seq, 1.0)
    bias_tabs = dilated_bias_tables(rel_bias)
    gate_col = MLA_Q_RANK + MLA_KV_RANK + 2 * LANE

    for layer in range(depth):
        w_small, w_dil = _layout_w_in(w_in[layer], d_model)
        h3 = prenorm_classes(x, mix_norm[layer], tm=TM)
        z = matmul_sigmoid_tail(h3, w_small, tm=TM, tn=TN_WIDE, sig_from=gate_col // TN_WIDE)
        qkvs = dilated_qkv_proj(h3, w_dil, batch=batch, seq=seq, tm=TM, tn=TN_WIDE)

        q = q_proj(z, q_norm[layer], _layout_w_uq(w_uq[layer]), q_tabs, seq=seq, tm=TQ, scale=q_scale)
        k, v = kv_proj(z, kv_norm[layer], _layout_w_ukv(w_ukv[layer]), k_tabs, seq=seq, tm=TQ)
        o_mla = mla_attention(q, k, v, batch=batch, seq=seq, tq=TQ_ATT)

        os_, lses = [], []
        for g in range(DIL_GROUPS):
            o_g, lse_g = dilated_group_attention(qkvs[g], bias_tabs[g], group=g)
            os_.append(o_g)
            lses.append(lse_g)
        o_dil = dilated_combine(os_, lses, seq=seq, tm=TM_COMBINE)

        merged = merge_branches(o_mla, o_dil, w_o_mla[layer].astype(BF16), w_o_dil[layer].astype(BF16),
                                z, gate_col=gate_col, tm=TM, tn=TN_WIDE)
        x = matmul_residual(merged, w_out[layer].astype(BF16), x, tm=TM_DOWN, tn=d_model)

        i = layer // 2
        if layer % 2 == 0:
            a = swiglu_up(x, ffn_norm[layer], w_ffn_gate[i].astype(BF16), w_ffn_up[i].astype(BF16),
                          tm=TM, tn=TN)
            x = matmul_residual(a, w_ffn_down[i].astype(BF16), x, tm=TM_DOWN, tn=TN_WIDE)
        else:
            x = moe_ffn(x, ffn_norm[layer], w_router[i], w_exp_gate[i].astype(BF16),
                        w_exp_up[i].astype(BF16), w_exp_down[i].astype(BF16))

    return rmsnorm(x, final_norm, tm=TM).reshape(batch, seq, d_model)
```

```python
import functools
import math

import numpy as np
import jax
import jax.numpy as jnp
from jax import lax
from jax.experimental import pallas as pl
from jax.experimental.pallas import tpu as pltpu

F32 = jnp.float32
BF16 = jnp.bfloat16

EPS = 1e-6
NEG = -1e30

MLA_HEADS = 16
MLA_Q_RANK = 512
MLA_KV_RANK = 256
MLA_NOPE = 128
MLA_ROPE = 64
MLA_V = 128
ROPE_THETA = 10000.0
DIL_PAIRS = ((128, 1), (512, 4), (2048, 16))
DIL_GROUPS = 3
DIL_HPG = 8
DIL_HEADS = DIL_GROUPS * DIL_HPG
DIL_DH = 128
DIL_BLOCK = 128
REL_BUCKETS = 32
REL_MAX_DIST = 2048
N_EXPERTS = 8
TOP_K = 2

LANE = 128
MLA_QK = 256
VMEM_LIMIT = 56 * 2**20

TM = 1024
TM_DOWN = 512
TN = 512
TN_WIDE = 1024
TN_FF = 1408
TM_MOE = 512
TQ = 512
TQ_ATT = 2048
MLA_ROW_RUN = 256
DIL_ROW_CHUNK = 64
TM_COMBINE = 512
GATHER_ROWS = 256
W_SPLIT_ROWS = 128
NORM_CHUNK = 256


def _cparams(*sem):
    return pltpu.CompilerParams(dimension_semantics=sem, vmem_limit_bytes=VMEM_LIMIT)


def _rms_rows(x, g):
    ms = jnp.mean(x * x, axis=-1, keepdims=True)
    return x * lax.rsqrt(ms + EPS) * g


def _norm_into(x_ref, g_ref, h_scr):
    rows = x_ref.shape[0]
    chunk = min(NORM_CHUNK, rows)
    for r in range(0, rows, chunk):
        x = x_ref[r:r + chunk, :].astype(F32)
        h_scr[r:r + chunk, :] = _rms_rows(x, g_ref[...]).astype(h_scr.dtype)


def _sigmoid(x):
    return 1.0 / (1.0 + jnp.exp(-x))


def _prenorm_body(x_ref, g_ref, o_ref, a_scr, b_scr):
    tm, k = x_ref.shape
    half = a_scr.shape[1]
    chunk = min(NORM_CHUNK, half)
    q4, q16 = half // 4, half // 16
    for s in range(tm // half):
        for r0 in range(0, half, chunk):
            rows = slice(s * half + r0, s * half + r0 + chunk)
            h = _rms_rows(x_ref[rows, :], g_ref[...])
            o_ref[0, rows, :] = h.astype(o_ref.dtype)
            for c in range(k // LANE):
                a_scr[c, r0:r0 + chunk, :] = h[:, c * LANE:(c + 1) * LANE]
        for c in range(k // LANE):
            lanes = slice(c * LANE, (c + 1) * LANE)
            for r in range(4):
                h4 = a_scr[c, pl.ds(r, q4, stride=4), :]
                b_scr[c, r * q4:(r + 1) * q4, :] = h4
                dst = r * (tm // 4) + s * q4
                o_ref[1, dst:dst + q4, lanes] = h4.astype(o_ref.dtype)
            for r in range(4):
                for r2 in range(4):
                    h16 = b_scr[c, pl.ds(r * q4 + r2, q16, stride=4), :]
                    dst = (r + 4 * r2) * (tm // 16) + s * q16
                    o_ref[2, dst:dst + q16, lanes] = h16.astype(o_ref.dtype)


def prenorm_classes(x, gain, *, tm):
    assert tuple(d for _, d in DIL_PAIRS) == (1, 4, 16)
    t, k = x.shape
    half = tm // 2
    return pl.pallas_call(
        _prenorm_body,
        grid=(t // tm,),
        in_specs=[pl.BlockSpec((tm, k), lambda i: (i, 0)), pl.BlockSpec((1, k), lambda i: (0, 0))],
        out_specs=pl.BlockSpec((DIL_GROUPS, tm, k), lambda i: (0, i, 0)),
        out_shape=jax.ShapeDtypeStruct((DIL_GROUPS, t, k), BF16),
        scratch_shapes=[pltpu.VMEM((k // LANE, half, LANE), F32), pltpu.VMEM((k // LANE, half, LANE), F32)],
        compiler_params=_cparams("parallel"),
        name="prenorm_classes",
    )(x, gain.reshape(1, k))


def _mm_sig_body(h_ref, w_ref, o_ref, *, sig_from):
    j = pl.program_id(1)
    acc = jnp.dot(h_ref[...], w_ref[...], preferred_element_type=F32)

    @pl.when(j < sig_from)
    def _():
        o_ref[...] = acc.astype(o_ref.dtype)

    @pl.when(j >= sig_from)
    def _():
        o_ref[...] = _sigmoid(acc).astype(o_ref.dtype)


def matmul_sigmoid_tail(h3, w, *, tm, tn, sig_from):
    _, t, k = h3.shape
    n = w.shape[1]
    return pl.pallas_call(
        functools.partial(_mm_sig_body, sig_from=sig_from),
        grid=(t // tm, n // tn),
        in_specs=[
            pl.BlockSpec((None, tm, k), lambda i, j: (0, i, 0)),
            pl.BlockSpec((k, tn), lambda i, j: (0, j)),
        ],
        out_specs=pl.BlockSpec((tm, tn), lambda i, j: (i, j)),
        out_shape=jax.ShapeDtypeStruct((t, n), BF16),
        compiler_params=_cparams("parallel", "arbitrary"),
        name="mixer_in_proj",
    )(h3, w)


def _dil_proj_body(h_ref, w_ref, o0_ref, o1_ref, o2_ref, *, tiles_per_group):
    tm = h_ref.shape[0]
    group = pl.program_id(1) // tiles_per_group
    acc = jnp.dot(h_ref[...], w_ref[...], preferred_element_type=F32)
    for g, o_ref in enumerate((o0_ref, o1_ref, o2_ref)):
        d = o_ref.shape[0]
        cs = tm // d

        @pl.when(group == g)
        def _(o_ref=o_ref, d=d, cs=cs):
            for r in range(d):
                o_ref[r] = acc[r * cs:(r + 1) * cs, :].astype(o_ref.dtype)


def dilated_qkv_proj(h3, w, *, batch, seq, tm, tn):
    _, t, k = h3.shape
    unit = DIL_HPG * DIL_DH
    upt = unit // tn
    tpg = 3 * upt
    per_seq = seq // tm

    def w_col(j):
        return ((j % tpg) // upt) * (DIL_GROUPS * upt) + (j // tpg) * upt + j % upt

    out_specs, out_shape = [], []
    for g, (_, d) in enumerate(DIL_PAIRS):
        out_specs.append(pl.BlockSpec(
            (None, d, tm // d, tn),
            lambda i, j, g=g: (i // per_seq, 0, i % per_seq, jnp.clip(j - g * tpg, 0, tpg - 1))))
        out_shape.append(jax.ShapeDtypeStruct((batch, d, seq // d, 3 * unit), BF16))
    return pl.pallas_call(
        functools.partial(_dil_proj_body, tiles_per_group=tpg),
        grid=(t // tm, DIL_GROUPS * tpg),
        in_specs=[
            pl.BlockSpec((None, tm, k), lambda i, j: (j // tpg, i, 0)),
            pl.BlockSpec((k, tn), lambda i, j: (0, w_col(j))),
        ],
        out_specs=out_specs,
        out_shape=out_shape,
        compiler_params=_cparams("parallel", "arbitrary"),
        name="dilated_qkv_proj",
    )(h3, w)


def _rope128(pe, c_ref, sa_ref, sb_ref):
    return (pe * c_ref[...]
            + pltpu.roll(pe, 96, 1) * sa_ref[...]
            + pltpu.roll(pe, 32, 1) * sb_ref[...])


def _q_proj_body(qa_ref, g_ref, w_ref, c_ref, sa_ref, sb_ref, o_ref, h_scr, *, scale):
    _norm_into(qa_ref, g_ref, h_scr)
    for h in range(MLA_HEADS):
        acc = jnp.dot(h_scr[...], w_ref[:, h * MLA_QK:(h + 1) * MLA_QK], preferred_element_type=F32)
        o_ref[:, h * MLA_QK:h * MLA_QK + LANE] = (acc[:, :LANE] * scale).astype(o_ref.dtype)
        pe = _rope128(acc[:, LANE:], c_ref, sa_ref, sb_ref)
        o_ref[:, h * MLA_QK + LANE:(h + 1) * MLA_QK] = pe.astype(o_ref.dtype)


def q_proj(z, gain, w, tabs, *, seq, tm, scale):
    t = z.shape[0]
    r = w.shape[0]
    n = w.shape[1]
    per_seq = seq // tm
    tab_spec = pl.BlockSpec((tm, LANE), lambda i: (i % per_seq, 0))
    return pl.pallas_call(
        functools.partial(_q_proj_body, scale=scale),
        grid=(t // tm,),
        in_specs=[
            pl.BlockSpec((tm, r), lambda i: (i, 0)),
            pl.BlockSpec((1, r), lambda i: (0, 0)),
            pl.BlockSpec((r, n), lambda i: (0, 0)),
            tab_spec, tab_spec, tab_spec,
        ],
        out_specs=pl.BlockSpec((tm, n), lambda i: (i, 0)),
        out_shape=jax.ShapeDtypeStruct((t, n), BF16),
        scratch_shapes=[pltpu.VMEM((tm, r), BF16)],
        compiler_params=_cparams("parallel"),
        name="mla_q_proj",
    )(z, gain.reshape(1, r), w, *tabs)


def _kv_proj_body(ckv_ref, kpe_ref, g_ref, w_ref, c_ref, sa_ref, sb_ref, k_ref, v_ref, h_scr):
    _norm_into(ckv_ref, g_ref, h_scr)
    pe = _rope128(kpe_ref[...].astype(F32), c_ref, sa_ref, sb_ref).astype(k_ref.dtype)
    nk = MLA_HEADS * MLA_NOPE
    for h in range(MLA_HEADS):
        kn = jnp.dot(h_scr[...], w_ref[:, h * MLA_NOPE:(h + 1) * MLA_NOPE], preferred_element_type=F32)
        k_ref[:, h * MLA_QK:h * MLA_QK + LANE] = kn.astype(k_ref.dtype)
        k_ref[:, h * MLA_QK + LANE:(h + 1) * MLA_QK] = pe
    v = jnp.dot(h_scr[...], w_ref[:, nk:], preferred_element_type=F32)
    v_ref[...] = v.astype(v_ref.dtype)


def kv_proj(z, gain, w, tabs, *, seq, tm):
    t = z.shape[0]
    r = w.shape[0]
    per_seq = seq // tm
    nv = MLA_HEADS * MLA_V
    tab_spec = pl.BlockSpec((tm, LANE), lambda i: (i % per_seq, 0))
    return pl.pallas_call(
        _kv_proj_body,
        grid=(t // tm,),
        in_specs=[
            pl.BlockSpec((tm, r), lambda i: (i, MLA_Q_RANK // r)),
            pl.BlockSpec((tm, LANE), lambda i: (i, (MLA_Q_RANK + r) // LANE)),
            pl.BlockSpec((1, r), lambda i: (0, 0)),
            pl.BlockSpec(w.shape, lambda i: (0, 0)),
            tab_spec, tab_spec, tab_spec,
        ],
        out_specs=[
            pl.BlockSpec((tm, MLA_HEADS * MLA_QK), lambda i: (i, 0)),
            pl.BlockSpec((tm, nv), lambda i: (i, 0)),
        ],
        out_shape=[
            jax.ShapeDtypeStruct((t, MLA_HEADS * MLA_QK), BF16),
            jax.ShapeDtypeStruct((t, nv), BF16),
        ],
        scratch_shapes=[pltpu.VMEM((tm, r), BF16)],
        compiler_params=_cparams("parallel"),
        name="mla_kv_proj",
    )(z, z, gain.reshape(1, r), w, *tabs)


def _dot_nt(a, b):
    return lax.dot_general(a, b, (((1,), (1,)), ((), ())), preferred_element_type=F32)


def _mla_attn_body(q_ref, k_ref, v_ref, o_ref, *, tq, tr):
    qi = pl.program_id(2)
    row = lax.broadcasted_iota(jnp.int32, (tr, tr), 0)
    col = lax.broadcasted_iota(jnp.int32, (tr, tr), 1)

    def run(first):
        rows = slice(first % tq, first % tq + tr)
        q = q_ref[rows, :]
        s_d = jnp.where(col <= row, _dot_nt(q, k_ref[first:first + tr, :]), NEG)
        m = jnp.max(s_d, axis=-1, keepdims=True)
        if first:
            s_o = _dot_nt(q, k_ref[:first, :])
            m = jnp.maximum(m, jnp.max(s_o, axis=-1, keepdims=True))
        p_d = jnp.exp2(s_d - m)
        l = jnp.sum(p_d, axis=-1, keepdims=True)
        o = jnp.dot(p_d.astype(v_ref.dtype), v_ref[first:first + tr, :], preferred_element_type=F32)
        if first:
            p_o = jnp.exp2(s_o - m)
            l = l + jnp.sum(p_o, axis=-1, keepdims=True)
            o = o + jnp.dot(p_o.astype(v_ref.dtype), v_ref[:first, :], preferred_element_type=F32)
        o_ref[rows, :] = (o / l).astype(o_ref.dtype)

    for n in range(k_ref.shape[0] // tq):
        @pl.when(qi == n)
        def _(n=n):
            for first in range(n * tq, (n + 1) * tq, tr):
                run(first)


def mla_attention(q, k, v, *, batch, seq, tq):
    t = batch * seq
    q3 = q.reshape(batch, seq, MLA_HEADS * MLA_QK)
    k3 = k.reshape(batch, seq, MLA_HEADS * MLA_QK)
    v3 = v.reshape(batch, seq, MLA_HEADS * MLA_V)
    out = pl.pallas_call(
        functools.partial(_mla_attn_body, tq=tq, tr=MLA_ROW_RUN),
        grid=(batch, MLA_HEADS, seq // tq),
        in_specs=[
            pl.BlockSpec((None, tq, MLA_QK), lambda b, h, i: (b, i, h)),
            pl.BlockSpec((None, seq, MLA_QK), lambda b, h, i: (b, 0, h)),
            pl.BlockSpec((None, seq, MLA_V), lambda b, h, i: (b, 0, h)),
        ],
        out_specs=pl.BlockSpec((None, tq, MLA_V), lambda b, h, i: (b, i, h)),
        out_shape=jax.ShapeDtypeStruct((batch, seq, MLA_HEADS * MLA_V), BF16),
        compiler_params=_cparams("parallel", "parallel", "arbitrary"),
        name="mla_attention",
    )(q3, k3, v3)
    return out.reshape(t, MLA_HEADS * MLA_V)


def _dil_attn_body(q_ref, kp_ref, kc_ref, vp_ref, vc_ref, b_ref, o_ref, lse_ref, s_scr, p_scr, m_scr,
                   *, scale, rc):
    blk = q_ref.shape[0]
    for h in range(DIL_HPG):
        sl = slice(h * DIL_DH, (h + 1) * DIL_DH)
        rows = slice(h * blk, (h + 1) * blk)
        q = q_ref[:, sl]
        s_scr[rows, :blk] = _dot_nt(q, kp_ref[:, sl]) * scale + b_ref[h, :, :blk]
        s_scr[rows, blk:] = _dot_nt(q, kc_ref[:, sl]) * scale + b_ref[h, :, blk:]
    for c in range(DIL_HPG * blk // rc):
        rows = slice(c * rc, (c + 1) * rc)
        s = s_scr[rows, :]
        m = jnp.max(s, axis=-1, keepdims=True)
        p_scr[rows, :] = jnp.exp(s - m).astype(p_scr.dtype)
        m_scr[rows, :] = jnp.broadcast_to(m, (rc, LANE))
    ones = jnp.ones((2 * blk, LANE), p_scr.dtype)
    lane = lax.broadcasted_iota(jnp.int32, (blk, LANE), 1)
    lse_tile = jnp.zeros((blk, LANE), F32)
    for h in range(DIL_HPG):
        sl = slice(h * DIL_DH, (h + 1) * DIL_DH)
        rows = slice(h * blk, (h + 1) * blk)
        o = (jnp.dot(p_scr[rows, :blk], vp_ref[:, sl], preferred_element_type=F32)
             + jnp.dot(p_scr[rows, blk:], vc_ref[:, sl], preferred_element_type=F32))
        l = jnp.dot(p_scr[rows, :], ones, preferred_element_type=F32)
        o_ref[:, sl] = (o / l).astype(o_ref.dtype)
        lse_tile = jnp.where(lane == h, m_scr[rows, :] + jnp.log(l), lse_tile)
    lse_ref[...] = lse_tile


def dilated_group_attention(qkv, bias, *, group):
    batch, dilation, ln, _ = qkv.shape
    hw = DIL_HPG * DIL_DH
    nb = ln // DIL_BLOCK

    def spec(col, prev):
        if prev:
            return pl.BlockSpec((None, None, DIL_BLOCK, hw), lambda b, r, n: (b, r, jnp.maximum(n - 1, 0), col))
        return pl.BlockSpec((None, None, DIL_BLOCK, hw), lambda b, r, n: (b, r, n, col))

    return pl.pallas_call(
        functools.partial(_dil_attn_body, scale=DIL_DH ** -0.5, rc=DIL_ROW_CHUNK),
        grid=(batch, dilation, nb),
        in_specs=[
            spec(0, False), spec(1, True), spec(1, False), spec(2, True), spec(2, False),
            pl.BlockSpec((None, DIL_HPG, DIL_BLOCK, 2 * DIL_BLOCK), lambda b, r, n: (jnp.minimum(n, 1), 0, 0, 0)),
        ],
        out_specs=[
            pl.BlockSpec((None, None, DIL_BLOCK, hw), lambda b, r, n: (b, r, n, 0)),
            pl.BlockSpec((None, None, DIL_BLOCK, LANE), lambda b, r, n: (b, r, n, 0)),
        ],
        out_shape=[
            jax.ShapeDtypeStruct((batch, dilation, ln, hw), BF16),
            jax.ShapeDtypeStruct((batch, dilation, ln, LANE), F32),
        ],
        scratch_shapes=[
            pltpu.VMEM((DIL_HPG * DIL_BLOCK, 2 * DIL_BLOCK), F32),
            pltpu.VMEM((DIL_HPG * DIL_BLOCK, 2 * DIL_BLOCK), BF16),
            pltpu.VMEM((DIL_HPG * DIL_BLOCK, LANE), F32),
        ],
        compiler_params=_cparams("parallel", "parallel", "arbitrary"),
        name=f"dilated_attention_g{group}",
    )(qkv, qkv, qkv, qkv, qkv, bias)


def _to_token_order(src_ref, dst_ref):
    d, cs, w = src_ref.shape
    for r in range(d):
        for c in range(w // LANE):
            dst_ref[c, pl.ds(r, cs, stride=d), :] = src_ref[r, :, c * LANE:(c + 1) * LANE].astype(dst_ref.dtype)


def _dil_combine_body(o0_ref, o1_ref, o2_ref, l0_ref, l1_ref, l2_ref, out_ref, o_scr, l_scr):
    for g, (o_ref, l_ref) in enumerate(((o1_ref, l1_ref), (o2_ref, l2_ref))):
        _to_token_order(o_ref, o_scr.at[g])
        _to_token_order(l_ref, l_scr.at[g])
    for h in range(DIL_HPG):
        sl = slice(h * DIL_DH, (h + 1) * DIL_DH)
        ls = [l0_ref[0, :, h:h + 1], l_scr[0, 0, :, h:h + 1], l_scr[1, 0, :, h:h + 1]]
        os_ = [o0_ref[0, :, sl].astype(F32), o_scr[0, h], o_scr[1, h]]
        m = jnp.maximum(jnp.maximum(ls[0], ls[1]), ls[2])
        es = [jnp.exp(x - m) for x in ls]
        den = es[0] + es[1] + es[2]
        acc = sum((e / den) * o for e, o in zip(es, os_))
        out_ref[:, sl] = acc.astype(out_ref.dtype)


def dilated_combine(os_, lses, *, seq, tm):
    batch = os_[0].shape[0]
    hw = os_[0].shape[-1]
    t = batch * seq
    per_seq = seq // tm

    def spec(arr):
        d, w = arr.shape[1], arr.shape[3]
        return pl.BlockSpec((None, d, tm // d, w), lambda i: (i // per_seq, 0, i % per_seq, 0))

    return pl.pallas_call(
        _dil_combine_body,
        grid=(t // tm,),
        in_specs=[spec(a) for a in (*os_, *lses)],
        out_specs=pl.BlockSpec((tm, hw), lambda i: (i, 0)),
        out_shape=jax.ShapeDtypeStruct((t, hw), BF16),
        scratch_shapes=[
            pltpu.VMEM((DIL_GROUPS - 1, hw // LANE, tm, LANE), F32),
            pltpu.VMEM((DIL_GROUPS - 1, 1, tm, LANE), F32),
        ],
        compiler_params=_cparams("parallel"),
        name="dilated_combine",
    )(*os_, *lses)


def _t5_bucket_np(dist):
    max_exact = REL_BUCKETS // 2
    large = max_exact + (np.log(np.maximum(dist, 1).astype(np.float32) / max_exact)
                         / math.log(REL_MAX_DIST / max_exact) * (REL_BUCKETS - max_exact)).astype(np.int32)
    large = np.minimum(large, REL_BUCKETS - 1)
    return np.where(dist < max_exact, dist, large)


def dilated_bias_tables(rel_bias):
    iq = np.arange(DIL_BLOCK)[:, None]
    jk = np.arange(2 * DIL_BLOCK)[None, :]
    dist_c = DIL_BLOCK + iq - jk
    tables = []
    for g, (window, dilation) in enumerate(DIL_PAIRS):
        span = window // dilation
        band = (dist_c >= 0) & (dist_c <= span)
        bucket = _t5_bucket_np(np.clip(dist_c, 0, None) * dilation)
        onehot = (bucket.reshape(-1, 1) == np.arange(REL_BUCKETS)[None, :]).astype(np.float32)
        heads = rel_bias[:, g * DIL_HPG:(g + 1) * DIL_HPG].astype(F32)
        bias = jnp.dot(onehot, heads, precision=lax.Precision.HIGHEST)
        bias = bias.reshape(DIL_BLOCK, 2 * DIL_BLOCK, DIL_HPG).transpose(2, 0, 1)
        general = jnp.where(band[None], bias, NEG)
        first = jnp.where((band & (jk >= DIL_BLOCK))[None], bias, NEG)
        tables.append(jnp.stack([first, general]))
    return tables


def _merge_body(a_ref, b_ref, wa_ref, wb_ref, ga_ref, gb_ref, o_ref):
    ya = jnp.dot(a_ref[...], wa_ref[...], preferred_element_type=F32)
    yb = jnp.dot(b_ref[...], wb_ref[...], preferred_element_type=F32)
    o_ref[...] = (ga_ref[...].astype(F32) * ya + gb_ref[...].astype(F32) * yb).astype(o_ref.dtype)


def merge_branches(o_a, o_b, w_a, w_b, z, *, gate_col, tm, tn):
    t, ka = o_a.shape
    kb = o_b.shape[1]
    n = w_a.shape[1]
    g0 = gate_col // tn
    return pl.pallas_call(
        _merge_body,
        grid=(t // tm, n // tn),
        in_specs=[
            pl.BlockSpec((tm, ka), lambda i, j: (i, 0)),
            pl.BlockSpec((tm, kb), lambda i, j: (i, 0)),
            pl.BlockSpec((ka, tn), lambda i, j: (0, j)),
            pl.BlockSpec((kb, tn), lambda i, j: (0, j)),
            pl.BlockSpec((tm, tn), lambda i, j: (i, g0 + j)),
            pl.BlockSpec((tm, tn), lambda i, j: (i, g0 + n // tn + j)),
        ],
        out_specs=pl.BlockSpec((tm, tn), lambda i, j: (i, j)),
        out_shape=jax.ShapeDtypeStruct((t, n), BF16),
        compiler_params=_cparams("parallel", "arbitrary"),
        name="merge_branches",
    )(o_a, o_b, w_a, w_b, z, z)


def _mm_residual_body(a_ref, w_ref, x_ref, o_ref):
    o_ref[...] = x_ref[...] + jnp.dot(a_ref[...], w_ref[...], preferred_element_type=F32)


def matmul_residual(a, w, x, *, tm, tn):
    t, k = a.shape
    n = w.shape[1]
    return pl.pallas_call(
        _mm_residual_body,
        grid=(t // tm, n // tn),
        in_specs=[
            pl.BlockSpec((tm, k), lambda i, j: (i, 0)),
            pl.BlockSpec((k, tn), lambda i, j: (0, j)),
            pl.BlockSpec((tm, tn), lambda i, j: (i, j)),
        ],
        out_specs=pl.BlockSpec((tm, tn), lambda i, j: (i, j)),
        out_shape=jax.ShapeDtypeStruct((t, n), F32),
        compiler_params=_cparams("parallel", "arbitrary"),
        name="matmul_residual",
    )(a, w, x)


def _swiglu_up(h, wg, wu):
    g = jnp.dot(h, wg, preferred_element_type=F32)
    u = jnp.dot(h, wu, preferred_element_type=F32)
    return g * _sigmoid(g) * u


def _swiglu_up_body(x_ref, g_ref, wg_ref, wu_ref, o_ref, h_scr):
    @pl.when(pl.program_id(1) == 0)
    def _():
        _norm_into(x_ref, g_ref, h_scr)

    o_ref[...] = _swiglu_up(h_scr[...], wg_ref[...], wu_ref[...]).astype(o_ref.dtype)


def swiglu_up(x, gain, w_gate, w_up, *, tm, tn):
    t, k = x.shape
    n = w_gate.shape[1]
    w_spec = pl.BlockSpec((k, tn), lambda i, j: (0, j))
    return pl.pallas_call(
        _swiglu_up_body,
        grid=(t // tm, n // tn),
        in_specs=[
            pl.BlockSpec((tm, k), lambda i, j: (i, 0)),
            pl.BlockSpec((1, k), lambda i, j: (0, 0)),
            w_spec, w_spec,
        ],
        out_specs=pl.BlockSpec((tm, tn), lambda i, j: (i, j)),
        out_shape=jax.ShapeDtypeStruct((t, n), BF16),
        scratch_shapes=[pltpu.VMEM((tm, k), BF16)],
        compiler_params=_cparams("parallel", "arbitrary"),
        name="swiglu_up",
    )(x, gain.reshape(1, k), w_gate, w_up)


def _grouped_up_body(te_ref, nu_ref, first_ref, next_ref, x_ref, g_ref, wg_ref, wu_ref, o_ref, buf, h_scr, sem,
                     *, nj):
    i = pl.program_id(0)
    j = pl.program_id(1)
    tm = o_ref.shape[0]
    part = tm // nj
    slot = i % 2

    @pl.when((i == 0) & (j == 0))
    def _():
        _start_row_gather(first_ref, x_ref, buf.at[0], sem.at[0])

    @pl.when(i < nu_ref[0])
    def _():
        @pl.when(j == 0)
        def _():
            _wait_row_gather(x_ref, buf.at[slot], sem.at[slot])
            _norm_into(buf.at[slot], g_ref, h_scr)

        _start_row_gather(next_ref, x_ref, buf.at[1 - slot], sem.at[1 - slot], first=j * part, count=part)
        o_ref[...] = _swiglu_up(h_scr[...], wg_ref[...], wu_ref[...]).astype(o_ref.dtype)

    @pl.when(i >= nu_ref[0])
    def _():
        @pl.when((i == nu_ref[0]) & (j == 0))
        def _():
            _wait_row_gather(x_ref, buf.at[slot], sem.at[slot])

        o_ref[...] = jnp.zeros_like(o_ref)


def grouped_swiglu_up(x, gain, row_token, w_gate, w_up, tile_expert, n_used, *, tm, tn):
    k = x.shape[1]
    r = row_token.shape[0]
    n = w_gate.shape[2]
    nj = n // tn
    n_tiles = r // tm
    idx3 = row_token.reshape(n_tiles, 1, tm)

    def col(i, j, nu):
        return jnp.where(i < nu[0], j, nj - 1)

    w_spec = pl.BlockSpec((None, k, tn), lambda i, j, te, nu: (te[i], 0, col(i, j, nu)))
    return pl.pallas_call(
        functools.partial(_grouped_up_body, nj=nj),
        grid_spec=pltpu.PrefetchScalarGridSpec(
            num_scalar_prefetch=2,
            grid=(n_tiles, nj),
            in_specs=[
                pl.BlockSpec((1, 1, tm), lambda i, j, te, nu: (0, 0, 0), memory_space=pltpu.SMEM),
                pl.BlockSpec((1, 1, tm), lambda i, j, te, nu: (jnp.minimum(i + 1, n_tiles - 1), 0, 0),
                             memory_space=pltpu.SMEM),
                pl.BlockSpec(memory_space=pl.ANY),
                pl.BlockSpec((1, k), lambda i, j, te, nu: (0, 0)),
                w_spec, w_spec,
            ],
            out_specs=pl.BlockSpec((tm, tn), lambda i, j, te, nu: (i, j)),
            scratch_shapes=[
                pltpu.VMEM((2, tm, k), x.dtype),
                pltpu.VMEM((tm, k), BF16),
                pltpu.SemaphoreType.DMA((2,)),
            ],
        ),
        out_shape=jax.ShapeDtypeStruct((r, n), BF16),
        compiler_params=_cparams("arbitrary", "arbitrary"),
        name="grouped_swiglu_up",
    )(tile_expert, n_used, idx3, idx3, x, gain.reshape(1, k), w_gate, w_up)


def _grouped_down_body(te_ref, nu_ref, a_ref, w_ref, o_ref):
    i = pl.program_id(0)

    @pl.when(i < nu_ref[0])
    def _():
        o_ref[...] = jnp.dot(a_ref[...], w_ref[...], preferred_element_type=F32)

    @pl.when(i >= nu_ref[0])
    def _():
        o_ref[...] = jnp.zeros_like(o_ref)


def grouped_down(a, w_down, tile_expert, n_used, *, tm, tn):
    r, k = a.shape
    n = w_down.shape[2]
    nj = n // tn

    def row(i, nu):
        return jnp.minimum(i, nu[0] - 1)

    def col(i, j, nu):
        return jnp.where(i < nu[0], j, nj - 1)

    return pl.pallas_call(
        _grouped_down_body,
        grid_spec=pltpu.PrefetchScalarGridSpec(
            num_scalar_prefetch=2,
            grid=(r // tm, nj),
            in_specs=[
                pl.BlockSpec((tm, k), lambda i, j, te, nu: (row(i, nu), 0)),
                pl.BlockSpec((None, k, tn), lambda i, j, te, nu: (te[i], 0, col(i, j, nu))),
            ],
            out_specs=pl.BlockSpec((tm, tn), lambda i, j, te, nu: (i, j)),
        ),
        out_shape=jax.ShapeDtypeStruct((r, n), F32),
        compiler_params=_cparams("arbitrary", "arbitrary"),
        name="grouped_down",
    )(tile_expert, n_used, a, w_down)


def _router_body(x_ref, g_ref, wr_ref, idx_ref, wt_ref):
    h = _rms_rows(x_ref[...], g_ref[...])
    logits = lax.dot_general(wr_ref[...], h, (((1,), (1,)), ((), ())),
                             precision=lax.Precision.HIGHEST, preferred_element_type=F32)
    e = lax.broadcasted_iota(jnp.int32, logits.shape, 0)
    n_e = logits.shape[0]
    m1 = jnp.max(logits, axis=0, keepdims=True)
    i1 = jnp.min(jnp.where(logits == m1, e, n_e), axis=0, keepdims=True)
    rest = jnp.where(e == i1, -jnp.inf, logits)
    m2 = jnp.max(rest, axis=0, keepdims=True)
    i2 = jnp.min(jnp.where(rest == m2, e, n_e), axis=0, keepdims=True)
    t = jnp.exp(m2 - m1)
    idx_ref[0:1, :] = i1
    idx_ref[1:2, :] = i2
    wt_ref[0:1, :] = 1.0 / (1.0 + t)
    wt_ref[1:2, :] = t / (1.0 + t)


def moe_router(x, gain, w_router_t, *, tm):
    t, k = x.shape
    n_e = w_router_t.shape[0]
    out_spec = pl.BlockSpec((TOP_K, tm), lambda i: (0, i))
    return pl.pallas_call(
        _router_body,
        grid=(t // tm,),
        in_specs=[
            pl.BlockSpec((tm, k), lambda i: (i, 0)),
            pl.BlockSpec((1, k), lambda i: (0, 0)),
            pl.BlockSpec((n_e, k), lambda i: (0, 0)),
        ],
        out_specs=[out_spec, out_spec],
        out_shape=[jax.ShapeDtypeStruct((TOP_K, t), jnp.int32), jax.ShapeDtypeStruct((TOP_K, t), F32)],
        compiler_params=_cparams("parallel"),
        name="moe_router",
    )(x, gain.reshape(1, k), w_router_t)


def _start_row_gather(idx_ref, src_ref, dst_ref, sem, first=0, count=None):
    count = dst_ref.shape[0] if count is None else count
    for r in range(count):
        row = first + r
        pltpu.make_async_copy(src_ref.at[pl.ds(idx_ref[0, 0, row], 1), :], dst_ref.at[pl.ds(row, 1), :], sem).start()


def _wait_row_gather(src_ref, dst_ref, sem):
    pltpu.make_async_copy(src_ref.at[pl.ds(0, dst_ref.shape[0]), :], dst_ref, sem).wait()


def _step_slots():
    i = pl.program_id(0)
    return i, pl.num_programs(0), i % 2


def _moe_combine_body(p0_ref, p1_ref, n0_ref, n1_ref, y_ref, x_ref, w_ref, o_ref, buf, sem):
    i, n, slot = _step_slots()

    def start(a_ref, b_ref, s):
        _start_row_gather(a_ref, y_ref, buf.at[s, 0], sem.at[s, 0])
        _start_row_gather(b_ref, y_ref, buf.at[s, 1], sem.at[s, 1])

    @pl.when(i == 0)
    def _():
        start(p0_ref, p1_ref, 0)

    @pl.when(i + 1 < n)
    def _():
        start(n0_ref, n1_ref, 1 - slot)

    _wait_row_gather(y_ref, buf.at[slot, 0], sem.at[slot, 0])
    _wait_row_gather(y_ref, buf.at[slot, 1], sem.at[slot, 1])
    w = w_ref[...]
    o_ref[...] = x_ref[...] + w[:, 0:1] * buf[slot, 0] + w[:, 1:2] * buf[slot, 1]


def moe_combine(x, y, pos, wts, *, tm):
    t, d = x.shape
    steps = t // tm
    pos3 = pos.reshape(TOP_K * steps, 1, tm)

    def idx_spec(choice, ahead):
        return pl.BlockSpec((1, 1, tm), lambda i: (choice * steps + jnp.minimum(i + ahead, steps - 1), 0, 0),
                            memory_space=pltpu.SMEM)

    return pl.pallas_call(
        _moe_combine_body,
        grid=(steps,),
        in_specs=[
            idx_spec(0, 0), idx_spec(1, 0), idx_spec(0, 1), idx_spec(1, 1),
            pl.BlockSpec(memory_space=pl.ANY),
            pl.BlockSpec((tm, d), lambda i: (i, 0)),
            pl.BlockSpec((tm, TOP_K), lambda i: (i, 0)),
        ],
        out_specs=pl.BlockSpec((tm, d), lambda i: (i, 0)),
        out_shape=jax.ShapeDtypeStruct((t, d), F32),
        scratch_shapes=[pltpu.VMEM((2, TOP_K, tm, d), y.dtype), pltpu.SemaphoreType.DMA((2, TOP_K))],
        compiler_params=_cparams("arbitrary"),
        name="moe_combine",
    )(pos3, pos3, pos3, pos3, y, x, wts)


def moe_dispatch_plan(top_idx, *, tm):
    k, t = top_idx.shape
    n_rows = k * t + N_EXPERTS * tm
    n_tiles = n_rows // tm
    e = top_idx.reshape(-1)
    onehot = (e[:, None] == jnp.arange(N_EXPERTS, dtype=jnp.int32)[None, :]).astype(jnp.int32)
    csum = jnp.cumsum(onehot, axis=0)
    counts = csum[-1]
    rank = jnp.sum((csum - onehot) * onehot, axis=1)
    padded = ((counts + tm - 1) // tm) * tm
    ends = jnp.cumsum(padded)
    starts = ends - padded
    pos = jnp.sum(onehot * starts[None, :], axis=1) + rank
    token = jnp.tile(jnp.arange(t, dtype=jnp.int32), k)
    row_token = jnp.zeros((n_rows,), jnp.int32).at[pos].set(token)
    n_used = (ends[-1] // tm).astype(jnp.int32)
    tile_start = jnp.minimum(jnp.arange(n_tiles, dtype=jnp.int32), n_used - 1) * tm
    tile_expert = jnp.sum((tile_start[:, None] >= ends[None, :]).astype(jnp.int32), axis=1)
    return row_token, pos.astype(jnp.int32), tile_expert.astype(jnp.int32), n_used.reshape(1)


def moe_ffn(x, gain, w_router, w_gate, w_up, w_down):
    idx, wts = moe_router(x, gain, w_router.T, tm=TM_MOE)
    row_token, pos, tile_expert, n_used = moe_dispatch_plan(idx, tm=TM_MOE)
    a = grouped_swiglu_up(x, gain, row_token, w_gate, w_up, tile_expert, n_used, tm=TM_MOE, tn=TN_FF)
    y = grouped_down(a, w_down, tile_expert, n_used, tm=TM_MOE, tn=TN_WIDE)
    return moe_combine(x, y, pos, wts.T, tm=GATHER_ROWS)


def _rmsnorm_body(x_ref, g_ref, o_ref):
    o_ref[...] = _rms_rows(x_ref[...], g_ref[...])


def rmsnorm(x, gain, *, tm):
    t, k = x.shape
    return pl.pallas_call(
        _rmsnorm_body,
        grid=(t // tm,),
        in_specs=[pl.BlockSpec((tm, k), lambda i: (i, 0)), pl.BlockSpec((1, k), lambda i: (0, 0))],
        out_specs=pl.BlockSpec((tm, k), lambda i: (i, 0)),
        out_shape=jax.ShapeDtypeStruct((t, k), F32),
        compiler_params=_cparams("parallel"),
        name="final_rmsnorm",
    )(x, gain.reshape(1, k))


def _rope_tables(seq, scale):
    half = MLA_ROPE // 2
    inv = ROPE_THETA ** (-jnp.arange(half, dtype=F32) / half)
    ang = jnp.arange(seq).astype(F32)[:, None] * inv[None, :]
    cos, sin = jnp.cos(ang) * scale, jnp.sin(ang) * scale
    z = jnp.zeros_like(cos)
    cat = lambda *p: jnp.concatenate(p, axis=1)
    return cat(cos, cos, z, z), cat(-sin, z, z, z), cat(z, sin, z, z)


def _split_w_in_body(w_ref, small_ref, dil_ref, *, off_dil, off_gate, step):
    gate_dst = off_dil + 2 * LANE - MLA_ROPE
    small_ref[:, :off_dil] = w_ref[:, :off_dil].astype(small_ref.dtype)
    small_ref[:, off_dil:gate_dst] = jnp.zeros((w_ref.shape[0], gate_dst - off_dil), small_ref.dtype)
    for c0 in range(0, w_ref.shape[1] - off_gate, step):
        small_ref[:, gate_dst + c0:gate_dst + c0 + step] = (
            w_ref[:, off_gate + c0:off_gate + c0 + step].astype(small_ref.dtype))
    for c0 in range(0, off_gate - off_dil, step):
        dil_ref[:, c0:c0 + step] = w_ref[:, off_dil + c0:off_dil + c0 + step].astype(dil_ref.dtype)


def split_w_in(w_in, layer, *, tk):
    _, k, n = w_in.shape
    off_dil = MLA_Q_RANK + MLA_KV_RANK + MLA_ROPE
    off_gate = off_dil + 3 * DIL_HEADS * DIL_DH
    n_small = off_dil + 2 * LANE - MLA_ROPE + n - off_gate
    return pl.pallas_call(
        functools.partial(_split_w_in_body, off_dil=off_dil, off_gate=off_gate, step=TN_WIDE),
        grid=(k // tk,),
        in_specs=[pl.BlockSpec((None, tk, n), lambda i: (layer, i, 0))],
        out_specs=[
            pl.BlockSpec((tk, n_small), lambda i: (i, 0)),
            pl.BlockSpec((tk, off_gate - off_dil), lambda i: (i, 0)),
        ],
        out_shape=[
            jax.ShapeDtypeStruct((k, n_small), BF16),
            jax.ShapeDtypeStruct((k, off_gate - off_dil), BF16),
        ],
        compiler_params=_cparams("parallel"),
        name="split_w_in",
    )(w_in)


def _layout_w_uq(w):
    r = w.shape[0]
    w = w.reshape(r, MLA_HEADS, MLA_NOPE + MLA_ROPE)
    w = jnp.pad(w, ((0, 0), (0, 0), (0, MLA_QK - MLA_NOPE - MLA_ROPE)))
    return w.reshape(r, MLA_HEADS * MLA_QK).astype(BF16)


def _layout_w_ukv(w):
    r = w.shape[0]
    w = w.reshape(r, MLA_HEADS, MLA_NOPE + MLA_V)
    k = w[:, :, :MLA_NOPE].reshape(r, MLA_HEADS * MLA_NOPE)
    v = w[:, :, MLA_NOPE:].reshape(r, MLA_HEADS * MLA_V)
    return jnp.concatenate([k, v], axis=1).astype(BF16)


def kernel(x, mix_norm, w_in, q_norm, w_uq, kv_norm, w_ukv, w_o_mla, w_o_dil, w_out, rel_bias,
           ffn_norm, w_ffn_gate, w_ffn_up, w_ffn_down, w_router, w_exp_gate, w_exp_up, w_exp_down,
           final_norm):
    batch, seq, d_model = x.shape
    depth = w_in.shape[0]
    t = batch * seq
    x = x.reshape(t, d_model)

    q_scale = (MLA_NOPE + MLA_ROPE) ** -0.5 * math.log2(math.e)
    q_tabs = _rope_tables(seq, q_scale)
    k_tabs = _rope_tables(seq, 1.0)
    bias_tabs = dilated_bias_tables(rel_bias)
    gate_col = MLA_Q_RANK + MLA_KV_RANK + 2 * LANE

    for layer in range(depth):
        w_small, w_dil = split_w_in(w_in, layer, tk=W_SPLIT_ROWS)
        h3 = prenorm_classes(x, mix_norm[layer], tm=TM)
        z = matmul_sigmoid_tail(h3, w_small, tm=TM, tn=TN_WIDE, sig_from=gate_col // TN_WIDE)
        qkvs = dilated_qkv_proj(h3, w_dil, batch=batch, seq=seq, tm=TM, tn=TN_WIDE)

        q = q_proj(z, q_norm[layer], _layout_w_uq(w_uq[layer]), q_tabs, seq=seq, tm=TQ, scale=q_scale)
        k, v = kv_proj(z, kv_norm[layer], _layout_w_ukv(w_ukv[layer]), k_tabs, seq=seq, tm=TQ)
        o_mla = mla_attention(q, k, v, batch=batch, seq=seq, tq=TQ_ATT)

        os_, lses = [], []
        for g in range(DIL_GROUPS):
            o_g, lse_g = dilated_group_attention(qkvs[g], bias_tabs[g], group=g)
            os_.append(o_g)
            lses.append(lse_g)
        o_dil = dilated_combine(os_, lses, seq=seq, tm=TM_COMBINE)

        merged = merge_branches(o_mla, o_dil, w_o_mla[layer].astype(BF16), w_o_dil[layer].astype(BF16),
                                z, gate_col=gate_col, tm=TM, tn=TN_WIDE)
        x = matmul_residual(merged, w_out[layer].astype(BF16), x, tm=TM_DOWN, tn=d_model)

        i = layer // 2
        if layer % 2 == 0:
            a = swiglu_up(x, ffn_norm[layer], w_ffn_gate[i].astype(BF16), w_ffn_up[i].astype(BF16),
                          tm=TM, tn=TN)
            x = matmul_residual(a, w_ffn_down[i].astype(BF16), x, tm=TM_DOWN, tn=TN_WIDE)
        else:
            x = moe_ffn(x, ffn_norm[layer], w_router[i], w_exp_gate[i].astype(BF16),
                        w_exp_up[i].astype(BF16), w_exp_down[i].astype(BF16))

    return rmsnorm(x, final_norm, tm=TM).reshape(batch, seq, d_model)
```

```python
import functools
import math

import numpy as np
import jax
import jax.numpy as jnp
from jax import lax
from jax.experimental import pallas as pl
from jax.experimental.pallas import tpu as pltpu

F32 = jnp.float32
BF16 = jnp.bfloat16

EPS = 1e-6
NEG = -1e30

MLA_HEADS = 16
MLA_Q_RANK = 512
MLA_KV_RANK = 256
MLA_NOPE = 128
MLA_ROPE = 64
MLA_V = 128
ROPE_THETA = 10000.0
DIL_PAIRS = ((128, 1), (512, 4), (2048, 16))
DIL_GROUPS = 3
DIL_HPG = 8
DIL_HEADS = DIL_GROUPS * DIL_HPG
DIL_DH = 128
DIL_BLOCK = 128
REL_BUCKETS = 32
REL_MAX_DIST = 2048
N_EXPERTS = 8
TOP_K = 2

LANE = 128
MLA_QK = 256
VMEM_LIMIT = 56 * 2**20

TM = 1024
TM_DOWN = 512
TN = 512
TN_WIDE = 1024
TN_FF = 1408
TM_MOE = 512
TQ = 512
TQ_ATT = 2048
MLA_ROW_RUN = 256
DIL_ROW_CHUNK = 64
TM_COMBINE = 512
GATHER_ROWS = 256
W_SPLIT_ROWS = 128
NORM_CHUNK = 256


def _cparams(*sem):
    return pltpu.CompilerParams(dimension_semantics=sem, vmem_limit_bytes=VMEM_LIMIT)


def _rms_rows(x, g):
    ms = jnp.mean(x * x, axis=-1, keepdims=True)
    return x * lax.rsqrt(ms + EPS) * g


def _norm_into(x_ref, g_ref, h_scr):
    rows = x_ref.shape[0]
    chunk = min(NORM_CHUNK, rows)
    for r in range(0, rows, chunk):
        x = x_ref[r:r + chunk, :].astype(F32)
        h_scr[r:r + chunk, :] = _rms_rows(x, g_ref[...]).astype(h_scr.dtype)


def _sigmoid(x):
    return 1.0 / (1.0 + jnp.exp(-x))


def _prenorm_body(x_ref, g_ref, o_ref, a_scr, b_scr):
    tm, k = x_ref.shape
    half = a_scr.shape[1]
    chunk = min(NORM_CHUNK, half)
    q4, q16 = half // 4, half // 16
    for s in range(tm // half):
        for r0 in range(0, half, chunk):
            rows = slice(s * half + r0, s * half + r0 + chunk)
            h = _rms_rows(x_ref[rows, :], g_ref[...])
            o_ref[0, rows, :] = h.astype(o_ref.dtype)
            for c in range(k // LANE):
                a_scr[c, r0:r0 + chunk, :] = h[:, c * LANE:(c + 1) * LANE]
        for c in range(k // LANE):
            lanes = slice(c * LANE, (c + 1) * LANE)
            for r in range(4):
                h4 = a_scr[c, pl.ds(r, q4, stride=4), :]
                b_scr[c, r * q4:(r + 1) * q4, :] = h4
                dst = r * (tm // 4) + s * q4
                o_ref[1, dst:dst + q4, lanes] = h4.astype(o_ref.dtype)
            for r in range(4):
                for r2 in range(4):
                    h16 = b_scr[c, pl.ds(r * q4 + r2, q16, stride=4), :]
                    dst = (r + 4 * r2) * (tm // 16) + s * q16
                    o_ref[2, dst:dst + q16, lanes] = h16.astype(o_ref.dtype)


def prenorm_classes(x, gain, *, tm):
    assert tuple(d for _, d in DIL_PAIRS) == (1, 4, 16)
    t, k = x.shape
    half = tm // 2
    return pl.pallas_call(
        _prenorm_body,
        grid=(t // tm,),
        in_specs=[pl.BlockSpec((tm, k), lambda i: (i, 0)), pl.BlockSpec((1, k), lambda i: (0, 0))],
        out_specs=pl.BlockSpec((DIL_GROUPS, tm, k), lambda i: (0, i, 0)),
        out_shape=jax.ShapeDtypeStruct((DIL_GROUPS, t, k), BF16),
        scratch_shapes=[pltpu.VMEM((k // LANE, half, LANE), F32), pltpu.VMEM((k // LANE, half, LANE), F32)],
        compiler_params=_cparams("parallel"),
        name="prenorm_classes",
    )(x, gain.reshape(1, k))


def _mm_sig_body(h_ref, w_ref, o_ref, *, sig_from):
    j = pl.program_id(1)
    acc = jnp.dot(h_ref[...], w_ref[...], preferred_element_type=F32)

    @pl.when(j < sig_from)
    def _():
        o_ref[...] = acc.astype(o_ref.dtype)

    @pl.when(j >= sig_from)
    def _():
        o_ref[...] = _sigmoid(acc).astype(o_ref.dtype)


def matmul_sigmoid_tail(h3, w, *, tm, tn, sig_from):
    _, t, k = h3.shape
    n = w.shape[1]
    return pl.pallas_call(
        functools.partial(_mm_sig_body, sig_from=sig_from),
        grid=(t // tm, n // tn),
        in_specs=[
            pl.BlockSpec((None, tm, k), lambda i, j: (0, i, 0)),
            pl.BlockSpec((k, tn), lambda i, j: (0, j)),
        ],
        out_specs=pl.BlockSpec((tm, tn), lambda i, j: (i, j)),
        out_shape=jax.ShapeDtypeStruct((t, n), BF16),
        compiler_params=_cparams("parallel", "arbitrary"),
        name="mixer_in_proj",
    )(h3, w)


def _dil_proj_body(h_ref, w_ref, o0_ref, o1_ref, o2_ref, *, tiles_per_group):
    tm = h_ref.shape[0]
    group = pl.program_id(1) // tiles_per_group
    acc = jnp.dot(h_ref[...], w_ref[...], preferred_element_type=F32)
    for g, o_ref in enumerate((o0_ref, o1_ref, o2_ref)):
        d = o_ref.shape[0]
        cs = tm // d

        @pl.when(group == g)
        def _(o_ref=o_ref, d=d, cs=cs):
            for r in range(d):
                o_ref[r] = acc[r * cs:(r + 1) * cs, :].astype(o_ref.dtype)


def dilated_qkv_proj(h3, w, *, batch, seq, tm, tn):
    _, t, k = h3.shape
    unit = DIL_HPG * DIL_DH
    upt = unit // tn
    tpg = 3 * upt
    per_seq = seq // tm

    def w_col(j):
        return ((j % tpg) // upt) * (DIL_GROUPS * upt) + (j // tpg) * upt + j % upt

    out_specs, out_shape = [], []
    for g, (_, d) in enumerate(DIL_PAIRS):
        out_specs.append(pl.BlockSpec(
            (None, d, tm // d, tn),
            lambda i, j, g=g: (i // per_seq, 0, i % per_seq, jnp.clip(j - g * tpg, 0, tpg - 1))))
        out_shape.append(jax.ShapeDtypeStruct((batch, d, seq // d, 3 * unit), BF16))
    return pl.pallas_call(
        functools.partial(_dil_proj_body, tiles_per_group=tpg),
        grid=(t // tm, DIL_GROUPS * tpg),
        in_specs=[
            pl.BlockSpec((None, tm, k), lambda i, j: (j // tpg, i, 0)),
            pl.BlockSpec((k, tn), lambda i, j: (0, w_col(j))),
        ],
        out_specs=out_specs,
        out_shape=out_shape,
        compiler_params=_cparams("parallel", "arbitrary"),
        name="dilated_qkv_proj",
    )(h3, w)


def _rope128(pe, c_ref, sa_ref, sb_ref):
    return (pe * c_ref[...]
            + pltpu.roll(pe, 96, 1) * sa_ref[...]
            + pltpu.roll(pe, 32, 1) * sb_ref[...])


def _q_proj_body(qa_ref, g_ref, w_ref, c_ref, sa_ref, sb_ref, o_ref, h_scr, *, scale):
    _norm_into(qa_ref, g_ref, h_scr)
    for h in range(MLA_HEADS):
        acc = jnp.dot(h_scr[...], w_ref[:, h * MLA_QK:(h + 1) * MLA_QK], preferred_element_type=F32)
        o_ref[:, h * MLA_QK:h * MLA_QK + LANE] = (acc[:, :LANE] * scale).astype(o_ref.dtype)
        pe = _rope128(acc[:, LANE:], c_ref, sa_ref, sb_ref)
        o_ref[:, h * MLA_QK + LANE:(h + 1) * MLA_QK] = pe.astype(o_ref.dtype)


def q_proj(z, gain, w, tabs, *, seq, tm, scale):
    t = z.shape[0]
    r = w.shape[0]
    n = w.shape[1]
    per_seq = seq // tm
    tab_spec = pl.BlockSpec((tm, LANE), lambda i: (i % per_seq, 0))
    return pl.pallas_call(
        functools.partial(_q_proj_body, scale=scale),
        grid=(t // tm,),
        in_specs=[
            pl.BlockSpec((tm, r), lambda i: (i, 0)),
            pl.BlockSpec((1, r), lambda i: (0, 0)),
            pl.BlockSpec((r, n), lambda i: (0, 0)),
            tab_spec, tab_spec, tab_spec,
        ],
        out_specs=pl.BlockSpec((tm, n), lambda i: (i, 0)),
        out_shape=jax.ShapeDtypeStruct((t, n), BF16),
        scratch_shapes=[pltpu.VMEM((tm, r), BF16)],
        compiler_params=_cparams("parallel"),
        name="mla_q_proj",
    )(z, gain.reshape(1, r), w, *tabs)


def _kv_proj_body(ckv_ref, kpe_ref, g_ref, w_ref, c_ref, sa_ref, sb_ref, k_ref, v_ref, h_scr):
    _norm_into(ckv_ref, g_ref, h_scr)
    pe = _rope128(kpe_ref[...].astype(F32), c_ref, sa_ref, sb_ref).astype(k_ref.dtype)
    nk = MLA_HEADS * MLA_NOPE
    for h in range(MLA_HEADS):
        kn = jnp.dot(h_scr[...], w_ref[:, h * MLA_NOPE:(h + 1) * MLA_NOPE], preferred_element_type=F32)
        k_ref[:, h * MLA_QK:h * MLA_QK + LANE] = kn.astype(k_ref.dtype)
        k_ref[:, h * MLA_QK + LANE:(h + 1) * MLA_QK] = pe
    v = jnp.dot(h_scr[...], w_ref[:, nk:], preferred_element_type=F32)
    v_ref[...] = v.astype(v_ref.dtype)


def kv_proj(z, gain, w, tabs, *, seq, tm):
    t = z.shape[0]
    r = w.shape[0]
    per_seq = seq // tm
    nv = MLA_HEADS * MLA_V
    tab_spec = pl.BlockSpec((tm, LANE), lambda i: (i % per_seq, 0))
    return pl.pallas_call(
        _kv_proj_body,
        grid=(t // tm,),
        in_specs=[
            pl.BlockSpec((tm, r), lambda i: (i, MLA_Q_RANK // r)),
            pl.BlockSpec((tm, LANE), lambda i: (i, (MLA_Q_RANK + r) // LANE)),
            pl.BlockSpec((1, r), lambda i: (0, 0)),
            pl.BlockSpec(w.shape, lambda i: (0, 0)),
            tab_spec, tab_spec, tab_spec,
        ],
        out_specs=[
            pl.BlockSpec((tm, MLA_HEADS * MLA_QK), lambda i: (i, 0)),
            pl.BlockSpec((tm, nv), lambda i: (i, 0)),
        ],
        out_shape=[
            jax.ShapeDtypeStruct((t, MLA_HEADS * MLA_QK), BF16),
            jax.ShapeDtypeStruct((t, nv), BF16),
        ],
        scratch_shapes=[pltpu.VMEM((tm, r), BF16)],
        compiler_params=_cparams("parallel"),
        name="mla_kv_proj",
    )(z, z, gain.reshape(1, r), w, *tabs)


def _dot_nt(a, b):
    return lax.dot_general(a, b, (((1,), (1,)), ((), ())), preferred_element_type=F32)


def _mla_attn_body(q_ref, k_ref, v_ref, o_ref, *, tq, tr):
    qi = pl.program_id(2)
    row = lax.broadcasted_iota(jnp.int32, (tr, tr), 0)
    col = lax.broadcasted_iota(jnp.int32, (tr, tr), 1)

    def run(first):
        rows = slice(first % tq, first % tq + tr)
        q = q_ref[rows, :]
        s_d = jnp.where(col <= row, _dot_nt(q, k_ref[first:first + tr, :]), NEG)
        m = jnp.max(s_d, axis=-1, keepdims=True)
        if first:
            s_o = _dot_nt(q, k_ref[:first, :])
            m = jnp.maximum(m, jnp.max(s_o, axis=-1, keepdims=True))
        p_d = jnp.exp2(s_d - m)
        l = jnp.sum(p_d, axis=-1, keepdims=True)
        o = jnp.dot(p_d.astype(v_ref.dtype), v_ref[first:first + tr, :], preferred_element_type=F32)
        if first:
            p_o = jnp.exp2(s_o - m)
            l = l + jnp.sum(p_o, axis=-1, keepdims=True)
            o = o + jnp.dot(p_o.astype(v_ref.dtype), v_ref[:first, :], preferred_element_type=F32)
        o_ref[rows, :] = (o / l).astype(o_ref.dtype)

    for n in range(k_ref.shape[0] // tq):
        @pl.when(qi == n)
        def _(n=n):
            for first in range(n * tq, (n + 1) * tq, tr):
                run(first)


def mla_attention(q, k, v, *, batch, seq, tq):
    t = batch * seq
    q3 = q.reshape(batch, seq, MLA_HEADS * MLA_QK)
    k3 = k.reshape(batch, seq, MLA_HEADS * MLA_QK)
    v3 = v.reshape(batch, seq, MLA_HEADS * MLA_V)
    out = pl.pallas_call(
        functools.partial(_mla_attn_body, tq=tq, tr=MLA_ROW_RUN),
        grid=(batch, MLA_HEADS, seq // tq),
        in_specs=[
            pl.BlockSpec((None, tq, MLA_QK), lambda b, h, i: (b, i, h)),
            pl.BlockSpec((None, seq, MLA_QK), lambda b, h, i: (b, 0, h)),
            pl.BlockSpec((None, seq, MLA_V), lambda b, h, i: (b, 0, h)),
        ],
        out_specs=pl.BlockSpec((None, tq, MLA_V), lambda b, h, i: (b, i, h)),
        out_shape=jax.ShapeDtypeStruct((batch, seq, MLA_HEADS * MLA_V), BF16),
        compiler_params=_cparams("parallel", "parallel", "arbitrary"),
        name="mla_attention",
    )(q3, k3, v3)
    return out.reshape(t, MLA_HEADS * MLA_V)


def _dil_attn_body(q_ref, kp_ref, kc_ref, vp_ref, vc_ref, b_ref, o_ref, lse_ref, s_scr, p_scr, m_scr,
                   *, scale, rc):
    blk = q_ref.shape[0]
    for h in range(DIL_HPG):
        sl = slice(h * DIL_DH, (h + 1) * DIL_DH)
        rows = slice(h * blk, (h + 1) * blk)
        q = q_ref[:, sl]
        s_scr[rows, :blk] = _dot_nt(q, kp_ref[:, sl]) * scale + b_ref[h, :, :blk]
        s_scr[rows, blk:] = _dot_nt(q, kc_ref[:, sl]) * scale + b_ref[h, :, blk:]
    for c in range(DIL_HPG * blk // rc):
        rows = slice(c * rc, (c + 1) * rc)
        s = s_scr[rows, :]
        m = jnp.max(s, axis=-1, keepdims=True)
        p_scr[rows, :] = jnp.exp(s - m).astype(p_scr.dtype)
        m_scr[rows, :] = jnp.broadcast_to(m, (rc, LANE))
    ones = jnp.ones((2 * blk, LANE), p_scr.dtype)
    lane = lax.broadcasted_iota(jnp.int32, (blk, LANE), 1)
    lse_tile = jnp.zeros((blk, LANE), F32)
    for h in range(DIL_HPG):
        sl = slice(h * DIL_DH, (h + 1) * DIL_DH)
        rows = slice(h * blk, (h + 1) * blk)
        o = (jnp.dot(p_scr[rows, :blk], vp_ref[:, sl], preferred_element_type=F32)
             + jnp.dot(p_scr[rows, blk:], vc_ref[:, sl], preferred_element_type=F32))
        l = jnp.dot(p_scr[rows, :], ones, preferred_element_type=F32)
        o_ref[:, sl] = (o / l).astype(o_ref.dtype)
        lse_tile = jnp.where(lane == h, m_scr[rows, :] + jnp.log(l), lse_tile)
    lse_ref[...] = lse_tile


def dilated_group_attention(qkv, bias, *, group):
    batch, dilation, ln, _ = qkv.shape
    hw = DIL_HPG * DIL_DH
    nb = ln // DIL_BLOCK

    def spec(col, prev):
        if prev:
            return pl.BlockSpec((None, None, DIL_BLOCK, hw), lambda b, r, n: (b, r, jnp.maximum(n - 1, 0), col))
        return pl.BlockSpec((None, None, DIL_BLOCK, hw), lambda b, r, n: (b, r, n, col))

    return pl.pallas_call(
        functools.partial(_dil_attn_body, scale=DIL_DH ** -0.5, rc=DIL_ROW_CHUNK),
        grid=(batch, dilation, nb),
        in_specs=[
            spec(0, False), spec(1, True), spec(1, False), spec(2, True), spec(2, False),
            pl.BlockSpec((None, DIL_HPG, DIL_BLOCK, 2 * DIL_BLOCK), lambda b, r, n: (jnp.minimum(n, 1), 0, 0, 0)),
        ],
        out_specs=[
            pl.BlockSpec((None, None, DIL_BLOCK, hw), lambda b, r, n: (b, r, n, 0)),
            pl.BlockSpec((None, None, DIL_BLOCK, LANE), lambda b, r, n: (b, r, n, 0)),
        ],
        out_shape=[
            jax.ShapeDtypeStruct((batch, dilation, ln, hw), BF16),
            jax.ShapeDtypeStruct((batch, dilation, ln, LANE), F32),
        ],
        scratch_shapes=[
            pltpu.VMEM((DIL_HPG * DIL_BLOCK, 2 * DIL_BLOCK), F32),
            pltpu.VMEM((DIL_HPG * DIL_BLOCK, 2 * DIL_BLOCK), BF16),
            pltpu.VMEM((DIL_HPG * DIL_BLOCK, LANE), F32),
        ],
        compiler_params=_cparams("parallel", "parallel", "arbitrary"),
        name=f"dilated_attention_g{group}",
    )(qkv, qkv, qkv, qkv, qkv, bias)


def _to_token_order(src_ref, dst_ref):
    d, cs, w = src_ref.shape
    for r in range(d):
        for c in range(w // LANE):
            dst_ref[c, pl.ds(r, cs, stride=d), :] = src_ref[r, :, c * LANE:(c + 1) * LANE].astype(dst_ref.dtype)


def _dil_combine_body(o0_ref, o1_ref, o2_ref, l0_ref, l1_ref, l2_ref, out_ref, o_scr, l_scr):
    for g, (o_ref, l_ref) in enumerate(((o1_ref, l1_ref), (o2_ref, l2_ref))):
        _to_token_order(o_ref, o_scr.at[g])
        _to_token_order(l_ref, l_scr.at[g])
    for h in range(DIL_HPG):
        sl = slice(h * DIL_DH, (h + 1) * DIL_DH)
        ls = [l0_ref[0, :, h:h + 1], l_scr[0, 0, :, h:h + 1], l_scr[1, 0, :, h:h + 1]]
        os_ = [o0_ref[0, :, sl].astype(F32), o_scr[0, h], o_scr[1, h]]
        m = jnp.maximum(jnp.maximum(ls[0], ls[1]), ls[2])
        es = [jnp.exp(x - m) for x in ls]
        den = es[0] + es[1] + es[2]
        acc = sum((e / den) * o for e, o in zip(es, os_))
        out_ref[:, sl] = acc.astype(out_ref.dtype)


def dilated_combine(os_, lses, *, seq, tm):
    batch = os_[0].shape[0]
    hw = os_[0].shape[-1]
    t = batch * seq
    per_seq = seq // tm

    def spec(arr):
        d, w = arr.shape[1], arr.shape[3]
        return pl.BlockSpec((None, d, tm // d, w), lambda i: (i // per_seq, 0, i % per_seq, 0))

    return pl.pallas_call(
        _dil_combine_body,
        grid=(t // tm,),
        in_specs=[spec(a) for a in (*os_, *lses)],
        out_specs=pl.BlockSpec((tm, hw), lambda i: (i, 0)),
        out_shape=jax.ShapeDtypeStruct((t, hw), BF16),
        scratch_shapes=[
            pltpu.VMEM((DIL_GROUPS - 1, hw // LANE, tm, LANE), F32),
            pltpu.VMEM((DIL_GROUPS - 1, 1, tm, LANE), F32),
        ],
        compiler_params=_cparams("parallel"),
        name="dilated_combine",
    )(*os_, *lses)


def _t5_bucket_np(dist):
    max_exact = REL_BUCKETS // 2
    large = max_exact + (np.log(np.maximum(dist, 1).astype(np.float32) / max_exact)
                         / math.log(REL_MAX_DIST / max_exact) * (REL_BUCKETS - max_exact)).astype(np.int32)
    large = np.minimum(large, REL_BUCKETS - 1)
    return np.where(dist < max_exact, dist, large)


def dilated_bias_tables(rel_bias):
    iq = np.arange(DIL_BLOCK)[:, None]
    jk = np.arange(2 * DIL_BLOCK)[None, :]
    dist_c = DIL_BLOCK + iq - jk
    tables = []
    for g, (window, dilation) in enumerate(DIL_PAIRS):
        span = window // dilation
        band = (dist_c >= 0) & (dist_c <= span)
        bucket = _t5_bucket_np(np.clip(dist_c, 0, None) * dilation)
        onehot = (bucket.reshape(-1, 1) == np.arange(REL_BUCKETS)[None, :]).astype(np.float32)
        heads = rel_bias[:, g * DIL_HPG:(g + 1) * DIL_HPG].astype(F32)
        bias = jnp.dot(onehot, heads, precision=lax.Precision.HIGHEST)
        bias = bias.reshape(DIL_BLOCK, 2 * DIL_BLOCK, DIL_HPG).transpose(2, 0, 1)
        general = jnp.where(band[None], bias, NEG)
        first = jnp.where((band & (jk >= DIL_BLOCK))[None], bias, NEG)
        tables.append(jnp.stack([first, general]))
    return tables


def _merge_body(a_ref, b_ref, wa_ref, wb_ref, ga_ref, gb_ref, o_ref):
    ya = jnp.dot(a_ref[...], wa_ref[...], preferred_element_type=F32)
    yb = jnp.dot(b_ref[...], wb_ref[...], preferred_element_type=F32)
    o_ref[...] = (ga_ref[...].astype(F32) * ya + gb_ref[...].astype(F32) * yb).astype(o_ref.dtype)


def merge_branches(o_a, o_b, w_a, w_b, z, *, gate_col, tm, tn):
    t, ka = o_a.shape
    kb = o_b.shape[1]
    n = w_a.shape[1]
    g0 = gate_col // tn
    return pl.pallas_call(
        _merge_body,
        grid=(t // tm, n // tn),
        in_specs=[
            pl.BlockSpec((tm, ka), lambda i, j: (i, 0)),
            pl.BlockSpec((tm, kb), lambda i, j: (i, 0)),
            pl.BlockSpec((ka, tn), lambda i, j: (0, j)),
            pl.BlockSpec((kb, tn), lambda i, j: (0, j)),
            pl.BlockSpec((tm, tn), lambda i, j: (i, g0 + j)),
            pl.BlockSpec((tm, tn), lambda i, j: (i, g0 + n // tn + j)),
        ],
        out_specs=pl.BlockSpec((tm, tn), lambda i, j: (i, j)),
        out_shape=jax.ShapeDtypeStruct((t, n), BF16),
        compiler_params=_cparams("parallel", "arbitrary"),
        name="merge_branches",
    )(o_a, o_b, w_a, w_b, z, z)


def _mm_residual_body(a_ref, w_ref, x_ref, o_ref):
    o_ref[...] = x_ref[...] + jnp.dot(a_ref[...], w_ref[...], preferred_element_type=F32)


def matmul_residual(a, w, x, *, tm, tn):
    t, k = a.shape
    n = w.shape[1]
    return pl.pallas_call(
        _mm_residual_body,
        grid=(t // tm, n // tn),
        in_specs=[
            pl.BlockSpec((tm, k), lambda i, j: (i, 0)),
            pl.BlockSpec((k, tn), lambda i, j: (0, j)),
            pl.BlockSpec((tm, tn), lambda i, j: (i, j)),
        ],
        out_specs=pl.BlockSpec((tm, tn), lambda i, j: (i, j)),
        out_shape=jax.ShapeDtypeStruct((t, n), F32),
        compiler_params=_cparams("parallel", "arbitrary"),
        name="matmul_residual",
    )(a, w, x)


def _swiglu_up(h, wg, wu):
    g = jnp.dot(h, wg, preferred_element_type=F32)
    u = jnp.dot(h, wu, preferred_element_type=F32)
    return g * _sigmoid(g) * u


def _swiglu_up_body(x_ref, g_ref, wg_ref, wu_ref, o_ref, h_scr):
    @pl.when(pl.program_id(1) == 0)
    def _():
        _norm_into(x_ref, g_ref, h_scr)

    o_ref[...] = _swiglu_up(h_scr[...], wg_ref[...], wu_ref[...]).astype(o_ref.dtype)


def swiglu_up(x, gain, w_gate, w_up, *, tm, tn):
    t, k = x.shape
    n = w_gate.shape[1]
    w_spec = pl.BlockSpec((k, tn), lambda i, j: (0, j))
    return pl.pallas_call(
        _swiglu_up_body,
        grid=(t // tm, n // tn),
        in_specs=[
            pl.BlockSpec((tm, k), lambda i, j: (i, 0)),
            pl.BlockSpec((1, k), lambda i, j: (0, 0)),
            w_spec, w_spec,
        ],
        out_specs=pl.BlockSpec((tm, tn), lambda i, j: (i, j)),
        out_shape=jax.ShapeDtypeStruct((t, n), BF16),
        scratch_shapes=[pltpu.VMEM((tm, k), BF16)],
        compiler_params=_cparams("parallel", "arbitrary"),
        name="swiglu_up",
    )(x, gain.reshape(1, k), w_gate, w_up)


def _grouped_up_body(te_ref, nu_ref, first_ref, next_ref, x_ref, g_ref, wg_ref, wu_ref, o_ref, buf, h_scr, sem,
                     *, nj):
    i = pl.program_id(0)
    j = pl.program_id(1)
    tm = o_ref.shape[0]
    part = tm // nj
    slot = i % 2

    @pl.when((i == 0) & (j == 0))
    def _():
        _start_row_gather(first_ref, x_ref, buf.at[0], sem.at[0])

    @pl.when(i < nu_ref[0])
    def _():
        @pl.when(j == 0)
        def _():
            _wait_row_gather(x_ref, buf.at[slot], sem.at[slot])
            _norm_into(buf.at[slot], g_ref, h_scr)

        _start_row_gather(next_ref, x_ref, buf.at[1 - slot], sem.at[1 - slot], first=j * part, count=part)
        o_ref[...] = _swiglu_up(h_scr[...], wg_ref[...], wu_ref[...]).astype(o_ref.dtype)

    @pl.when(i >= nu_ref[0])
    def _():
        @pl.when((i == nu_ref[0]) & (j == 0))
        def _():
            _wait_row_gather(x_ref, buf.at[slot], sem.at[slot])

        o_ref[...] = jnp.zeros_like(o_ref)


def grouped_swiglu_up(x, gain, row_token, w_gate, w_up, tile_expert, n_used, *, moe_layer, tm, tn):
    k = x.shape[1]
    r = row_token.shape[0]
    n = w_gate.shape[3]
    nj = n // tn
    n_tiles = r // tm
    idx3 = row_token.reshape(n_tiles, 1, tm)

    def col(i, j, nu):
        return jnp.where(i < nu[0], j, nj - 1)

    w_spec = pl.BlockSpec((None, None, k, tn), lambda i, j, te, nu: (moe_layer, te[i], 0, col(i, j, nu)))
    return pl.pallas_call(
        functools.partial(_grouped_up_body, nj=nj),
        grid_spec=pltpu.PrefetchScalarGridSpec(
            num_scalar_prefetch=2,
            grid=(n_tiles, nj),
            in_specs=[
                pl.BlockSpec((1, 1, tm), lambda i, j, te, nu: (0, 0, 0), memory_space=pltpu.SMEM),
                pl.BlockSpec((1, 1, tm), lambda i, j, te, nu: (jnp.minimum(i + 1, n_tiles - 1), 0, 0),
                             memory_space=pltpu.SMEM),
                pl.BlockSpec(memory_space=pl.ANY),
                pl.BlockSpec((1, k), lambda i, j, te, nu: (0, 0)),
                w_spec, w_spec,
            ],
            out_specs=pl.BlockSpec((tm, tn), lambda i, j, te, nu: (i, j)),
            scratch_shapes=[
                pltpu.VMEM((2, tm, k), x.dtype),
                pltpu.VMEM((tm, k), BF16),
                pltpu.SemaphoreType.DMA((2,)),
            ],
        ),
        out_shape=jax.ShapeDtypeStruct((r, n), BF16),
        compiler_params=_cparams("arbitrary", "arbitrary"),
        name="grouped_swiglu_up",
    )(tile_expert, n_used, idx3, idx3, x, gain.reshape(1, k), w_gate, w_up)


def _grouped_down_body(te_ref, nu_ref, a_ref, w_ref, o_ref):
    i = pl.program_id(0)

    @pl.when(i < nu_ref[0])
    def _():
        o_ref[...] = jnp.dot(a_ref[...], w_ref[...], preferred_element_type=F32)

    @pl.when(i >= nu_ref[0])
    def _():
        o_ref[...] = jnp.zeros_like(o_ref)


def grouped_down(a, w_down, tile_expert, n_used, *, moe_layer, tm, tn):
    r, k = a.shape
    n = w_down.shape[3]
    nj = n // tn

    def row(i, nu):
        return jnp.minimum(i, nu[0] - 1)

    def col(i, j, nu):
        return jnp.where(i < nu[0], j, nj - 1)

    return pl.pallas_call(
        _grouped_down_body,
        grid_spec=pltpu.PrefetchScalarGridSpec(
            num_scalar_prefetch=2,
            grid=(r // tm, nj),
            in_specs=[
                pl.BlockSpec((tm, k), lambda i, j, te, nu: (row(i, nu), 0)),
                pl.BlockSpec((None, None, k, tn), lambda i, j, te, nu: (moe_layer, te[i], 0, col(i, j, nu))),
            ],
            out_specs=pl.BlockSpec((tm, tn), lambda i, j, te, nu: (i, j)),
        ),
        out_shape=jax.ShapeDtypeStruct((r, n), F32),
        compiler_params=_cparams("arbitrary", "arbitrary"),
        name="grouped_down",
    )(tile_expert, n_used, a, w_down)


def _router_body(x_ref, g_ref, wr_ref, idx_ref, wt_ref):
    h = _rms_rows(x_ref[...], g_ref[...])
    logits = lax.dot_general(wr_ref[...], h, (((1,), (1,)), ((), ())),
                             precision=lax.Precision.HIGHEST, preferred_element_type=F32)
    e = lax.broadcasted_iota(jnp.int32, logits.shape, 0)
    n_e = logits.shape[0]
    m1 = jnp.max(logits, axis=0, keepdims=True)
    i1 = jnp.min(jnp.where(logits == m1, e, n_e), axis=0, keepdims=True)
    rest = jnp.where(e == i1, -jnp.inf, logits)
    m2 = jnp.max(rest, axis=0, keepdims=True)
    i2 = jnp.min(jnp.where(rest == m2, e, n_e), axis=0, keepdims=True)
    t = jnp.exp(m2 - m1)
    idx_ref[0:1, :] = i1
    idx_ref[1:2, :] = i2
    wt_ref[0:1, :] = 1.0 / (1.0 + t)
    wt_ref[1:2, :] = t / (1.0 + t)


def moe_router(x, gain, w_router_t, *, tm):
    t, k = x.shape
    n_e = w_router_t.shape[0]
    out_spec = pl.BlockSpec((TOP_K, tm), lambda i: (0, i))
    return pl.pallas_call(
        _router_body,
        grid=(t // tm,),
        in_specs=[
            pl.BlockSpec((tm, k), lambda i: (i, 0)),
            pl.BlockSpec((1, k), lambda i: (0, 0)),
            pl.BlockSpec((n_e, k), lambda i: (0, 0)),
        ],
        out_specs=[out_spec, out_spec],
        out_shape=[jax.ShapeDtypeStruct((TOP_K, t), jnp.int32), jax.ShapeDtypeStruct((TOP_K, t), F32)],
        compiler_params=_cparams("parallel"),
        name="moe_router",
    )(x, gain.reshape(1, k), w_router_t)


def _start_row_gather(idx_ref, src_ref, dst_ref, sem, first=0, count=None):
    count = dst_ref.shape[0] if count is None else count
    for r in range(count):
        row = first + r
        pltpu.make_async_copy(src_ref.at[pl.ds(idx_ref[0, 0, row], 1), :], dst_ref.at[pl.ds(row, 1), :], sem).start()


def _wait_row_gather(src_ref, dst_ref, sem):
    pltpu.make_async_copy(src_ref.at[pl.ds(0, dst_ref.shape[0]), :], dst_ref, sem).wait()


def _step_slots():
    i = pl.program_id(0)
    return i, pl.num_programs(0), i % 2


def _moe_combine_body(p0_ref, p1_ref, n0_ref, n1_ref, y_ref, x_ref, w_ref, o_ref, buf, sem):
    i, n, slot = _step_slots()

    def start(a_ref, b_ref, s):
        _start_row_gather(a_ref, y_ref, buf.at[s, 0], sem.at[s, 0])
        _start_row_gather(b_ref, y_ref, buf.at[s, 1], sem.at[s, 1])

    @pl.when(i == 0)
    def _():
        start(p0_ref, p1_ref, 0)

    @pl.when(i + 1 < n)
    def _():
        start(n0_ref, n1_ref, 1 - slot)

    _wait_row_gather(y_ref, buf.at[slot, 0], sem.at[slot, 0])
    _wait_row_gather(y_ref, buf.at[slot, 1], sem.at[slot, 1])
    w = w_ref[...]
    o_ref[...] = x_ref[...] + w[:, 0:1] * buf[slot, 0] + w[:, 1:2] * buf[slot, 1]


def moe_combine(x, y, pos, wts, *, tm):
    t, d = x.shape
    steps = t // tm
    pos3 = pos.reshape(TOP_K * steps, 1, tm)

    def idx_spec(choice, ahead):
        return pl.BlockSpec((1, 1, tm), lambda i: (choice * steps + jnp.minimum(i + ahead, steps - 1), 0, 0),
                            memory_space=pltpu.SMEM)

    return pl.pallas_call(
        _moe_combine_body,
        grid=(steps,),
        in_specs=[
            idx_spec(0, 0), idx_spec(1, 0), idx_spec(0, 1), idx_spec(1, 1),
            pl.BlockSpec(memory_space=pl.ANY),
            pl.BlockSpec((tm, d), lambda i: (i, 0)),
            pl.BlockSpec((tm, TOP_K), lambda i: (i, 0)),
        ],
        out_specs=pl.BlockSpec((tm, d), lambda i: (i, 0)),
        out_shape=jax.ShapeDtypeStruct((t, d), F32),
        scratch_shapes=[pltpu.VMEM((2, TOP_K, tm, d), y.dtype), pltpu.SemaphoreType.DMA((2, TOP_K))],
        compiler_params=_cparams("arbitrary"),
        name="moe_combine",
    )(pos3, pos3, pos3, pos3, y, x, wts)


def moe_dispatch_plan(top_idx, *, tm):
    k, t = top_idx.shape
    n_rows = k * t + N_EXPERTS * tm
    n_tiles = n_rows // tm
    e = top_idx.reshape(-1)
    onehot = (e[:, None] == jnp.arange(N_EXPERTS, dtype=jnp.int32)[None, :]).astype(jnp.int32)
    csum = jnp.cumsum(onehot, axis=0)
    counts = csum[-1]
    rank = jnp.sum((csum - onehot) * onehot, axis=1)
    padded = ((counts + tm - 1) // tm) * tm
    ends = jnp.cumsum(padded)
    starts = ends - padded
    pos = jnp.sum(onehot * starts[None, :], axis=1) + rank
    token = jnp.tile(jnp.arange(t, dtype=jnp.int32), k)
    row_token = jnp.zeros((n_rows,), jnp.int32).at[pos].set(token)
    n_used = (ends[-1] // tm).astype(jnp.int32)
    tile_start = jnp.minimum(jnp.arange(n_tiles, dtype=jnp.int32), n_used - 1) * tm
    tile_expert = jnp.sum((tile_start[:, None] >= ends[None, :]).astype(jnp.int32), axis=1)
    return row_token, pos.astype(jnp.int32), tile_expert.astype(jnp.int32), n_used.reshape(1)


def moe_ffn(x, gain, w_router, w_gate, w_up, w_down, moe_layer):
    idx, wts = moe_router(x, gain, w_router.T, tm=TM_MOE)
    row_token, pos, tile_expert, n_used = moe_dispatch_plan(idx, tm=TM_MOE)
    a = grouped_swiglu_up(x, gain, row_token, w_gate, w_up, tile_expert, n_used,
                          moe_layer=moe_layer, tm=TM_MOE, tn=TN_FF)
    y = grouped_down(a, w_down, tile_expert, n_used, moe_layer=moe_layer, tm=TM_MOE, tn=TN_WIDE)
    return moe_combine(x, y, pos, wts.T, tm=GATHER_ROWS)


def _rmsnorm_body(x_ref, g_ref, o_ref):
    o_ref[...] = _rms_rows(x_ref[...], g_ref[...])


def rmsnorm(x, gain, *, tm):
    t, k = x.shape
    return pl.pallas_call(
        _rmsnorm_body,
        grid=(t // tm,),
        in_specs=[pl.BlockSpec((tm, k), lambda i: (i, 0)), pl.BlockSpec((1, k), lambda i: (0, 0))],
        out_specs=pl.BlockSpec((tm, k), lambda i: (i, 0)),
        out_shape=jax.ShapeDtypeStruct((t, k), F32),
        compiler_params=_cparams("parallel"),
        name="final_rmsnorm",
    )(x, gain.reshape(1, k))


def _rope_tables(seq, scale):
    half = MLA_ROPE // 2
    inv = ROPE_THETA ** (-jnp.arange(half, dtype=F32) / half)
    ang = jnp.arange(seq).astype(F32)[:, None] * inv[None, :]
    cos, sin = jnp.cos(ang) * scale, jnp.sin(ang) * scale
    z = jnp.zeros_like(cos)
    cat = lambda *p: jnp.concatenate(p, axis=1)
    return cat(cos, cos, z, z), cat(-sin, z, z, z), cat(z, sin, z, z)


def _split_w_in_body(w_ref, small_ref, dil_ref, *, off_dil, off_gate, step):
    gate_dst = off_dil + 2 * LANE - MLA_ROPE
    small_ref[:, :off_dil] = w_ref[:, :off_dil].astype(small_ref.dtype)
    small_ref[:, off_dil:gate_dst] = jnp.zeros((w_ref.shape[0], gate_dst - off_dil), small_ref.dtype)
    for c0 in range(0, w_ref.shape[1] - off_gate, step):
        small_ref[:, gate_dst + c0:gate_dst + c0 + step] = (
            w_ref[:, off_gate + c0:off_gate + c0 + step].astype(small_ref.dtype))
    for c0 in range(0, off_gate - off_dil, step):
        dil_ref[:, c0:c0 + step] = w_ref[:, off_dil + c0:off_dil + c0 + step].astype(dil_ref.dtype)


def split_w_in(w_in, layer, *, tk):
    _, k, n = w_in.shape
    off_dil = MLA_Q_RANK + MLA_KV_RANK + MLA_ROPE
    off_gate = off_dil + 3 * DIL_HEADS * DIL_DH
    n_small = off_dil + 2 * LANE - MLA_ROPE + n - off_gate
    return pl.pallas_call(
        functools.partial(_split_w_in_body, off_dil=off_dil, off_gate=off_gate, step=TN_WIDE),
        grid=(k // tk,),
        in_specs=[pl.BlockSpec((None, tk, n), lambda i: (layer, i, 0))],
        out_specs=[
            pl.BlockSpec((tk, n_small), lambda i: (i, 0)),
            pl.BlockSpec((tk, off_gate - off_dil), lambda i: (i, 0)),
        ],
        out_shape=[
            jax.ShapeDtypeStruct((k, n_small), BF16),
            jax.ShapeDtypeStruct((k, off_gate - off_dil), BF16),
        ],
        compiler_params=_cparams("parallel"),
        name="split_w_in",
    )(w_in)


def _layout_w_uq(w):
    r = w.shape[0]
    w = w.reshape(r, MLA_HEADS, MLA_NOPE + MLA_ROPE)
    w = jnp.pad(w, ((0, 0), (0, 0), (0, MLA_QK - MLA_NOPE - MLA_ROPE)))
    return w.reshape(r, MLA_HEADS * MLA_QK).astype(BF16)


def _layout_w_ukv(w):
    r = w.shape[0]
    w = w.reshape(r, MLA_HEADS, MLA_NOPE + MLA_V)
    k = w[:, :, :MLA_NOPE].reshape(r, MLA_HEADS * MLA_NOPE)
    v = w[:, :, MLA_NOPE:].reshape(r, MLA_HEADS * MLA_V)
    return jnp.concatenate([k, v], axis=1).astype(BF16)


def kernel(x, mix_norm, w_in, q_norm, w_uq, kv_norm, w_ukv, w_o_mla, w_o_dil, w_out, rel_bias,
           ffn_norm, w_ffn_gate, w_ffn_up, w_ffn_down, w_router, w_exp_gate, w_exp_up, w_exp_down,
           final_norm):
    batch, seq, d_model = x.shape
    depth = w_in.shape[0]
    t = batch * seq
    x = x.reshape(t, d_model)

    q_scale = (MLA_NOPE + MLA_ROPE) ** -0.5 * math.log2(math.e)
    q_tabs = _rope_tables(seq, q_scale)
    k_tabs = _rope_tables(seq, 1.0)
    bias_tabs = dilated_bias_tables(rel_bias)
    gate_col = MLA_Q_RANK + MLA_KV_RANK + 2 * LANE
    exp_gate, exp_up, exp_down = (w.astype(BF16) for w in (w_exp_gate, w_exp_up, w_exp_down))

    for layer in range(depth):
        w_small, w_dil = split_w_in(w_in, layer, tk=W_SPLIT_ROWS)
        h3 = prenorm_classes(x, mix_norm[layer], tm=TM)
        z = matmul_sigmoid_tail(h3, w_small, tm=TM, tn=TN_WIDE, sig_from=gate_col // TN_WIDE)
        qkvs = dilated_qkv_proj(h3, w_dil, batch=batch, seq=seq, tm=TM, tn=TN_WIDE)

        q = q_proj(z, q_norm[layer], _layout_w_uq(w_uq[layer]), q_tabs, seq=seq, tm=TQ, scale=q_scale)
        k, v = kv_proj(z, kv_norm[layer], _layout_w_ukv(w_ukv[layer]), k_tabs, seq=seq, tm=TQ)
        o_mla = mla_attention(q, k, v, batch=batch, seq=seq, tq=TQ_ATT)

        os_, lses = [], []
        for g in range(DIL_GROUPS):
            o_g, lse_g = dilated_group_attention(qkvs[g], bias_tabs[g], group=g)
            os_.append(o_g)
            lses.append(lse_g)
        o_dil = dilated_combine(os_, lses, seq=seq, tm=TM_COMBINE)

        merged = merge_branches(o_mla, o_dil, w_o_mla[layer].astype(BF16), w_o_dil[layer].astype(BF16),
                                z, gate_col=gate_col, tm=TM, tn=TN_WIDE)
        x = matmul_residual(merged, w_out[layer].astype(BF16), x, tm=TM_DOWN, tn=d_model)

        i = layer // 2
        if layer % 2 == 0:
            a = swiglu_up(x, ffn_norm[layer], w_ffn_gate[i].astype(BF16), w_ffn_up[i].astype(BF16),
                          tm=TM, tn=TN)
            x = matmul_residual(a, w_ffn_down[i].astype(BF16), x, tm=TM_DOWN, tn=TN_WIDE)
        else:
            x = moe_ffn(x, ffn_norm[layer], w_router[i], exp_gate, exp_up, exp_down, i)

    return rmsnorm(x, final_norm, tm=TM).reshape(batch, seq, d_model)
```

```python
import functools
import math

import numpy as np
import jax
import jax.numpy as jnp
from jax import lax
from jax.experimental import pallas as pl
from jax.experimental.pallas import tpu as pltpu

F32 = jnp.float32
BF16 = jnp.bfloat16

EPS = 1e-6
NEG = -1e30

MLA_HEADS = 16
MLA_Q_RANK = 512
MLA_KV_RANK = 256
MLA_NOPE = 128
MLA_ROPE = 64
MLA_V = 128
ROPE_THETA = 10000.0
DIL_PAIRS = ((128, 1), (512, 4), (2048, 16))
DIL_GROUPS = 3
DIL_HPG = 8
DIL_HEADS = DIL_GROUPS * DIL_HPG
DIL_DH = 128
DIL_BLOCK = 128
REL_BUCKETS = 32
REL_MAX_DIST = 2048
N_EXPERTS = 8
TOP_K = 2

LANE = 128
MLA_QK = 256
VMEM_LIMIT = 56 * 2**20

TM = 1024
TM_DOWN = 512
TN = 512
TN_WIDE = 1024
TN_FF = 1408
TM_MOE = 512
TQ = 512
TQ_ATT = 2048
MLA_ROW_RUN = 256
DIL_ROW_CHUNK = 64
TM_COMBINE = 512
GATHER_ROWS = 256
W_SPLIT_ROWS = 128
NORM_CHUNK = 256


def _cparams(*sem):
    return pltpu.CompilerParams(dimension_semantics=sem, vmem_limit_bytes=VMEM_LIMIT)


def _rms_rows(x, g):
    ms = jnp.mean(x * x, axis=-1, keepdims=True)
    return x * lax.rsqrt(ms + EPS) * g


def _norm_into(x_ref, g_ref, h_scr):
    rows = x_ref.shape[0]
    chunk = min(NORM_CHUNK, rows)
    for r in range(0, rows, chunk):
        x = x_ref[r:r + chunk, :].astype(F32)
        h_scr[r:r + chunk, :] = _rms_rows(x, g_ref[...]).astype(h_scr.dtype)


def _sigmoid(x):
    return 1.0 / (1.0 + jnp.exp(-x))


def _prenorm_body(x_ref, g_ref, o_ref, a_scr, b_scr):
    tm, k = x_ref.shape
    half = a_scr.shape[1]
    chunk = min(NORM_CHUNK, half)
    q4, q16 = half // 4, half // 16
    for s in range(tm // half):
        for r0 in range(0, half, chunk):
            rows = slice(s * half + r0, s * half + r0 + chunk)
            h = _rms_rows(x_ref[rows, :], g_ref[...])
            o_ref[0, rows, :] = h.astype(o_ref.dtype)
            for c in range(k // LANE):
                a_scr[c, r0:r0 + chunk, :] = h[:, c * LANE:(c + 1) * LANE]
        for c in range(k // LANE):
            lanes = slice(c * LANE, (c + 1) * LANE)
            for r in range(4):
                h4 = a_scr[c, pl.ds(r, q4, stride=4), :]
                b_scr[c, r * q4:(r + 1) * q4, :] = h4
                dst = r * (tm // 4) + s * q4
                o_ref[1, dst:dst + q4, lanes] = h4.astype(o_ref.dtype)
            for r in range(4):
                for r2 in range(4):
                    h16 = b_scr[c, pl.ds(r * q4 + r2, q16, stride=4), :]
                    dst = (r + 4 * r2) * (tm // 16) + s * q16
                    o_ref[2, dst:dst + q16, lanes] = h16.astype(o_ref.dtype)


def prenorm_classes(x, gain, *, tm):
    assert tuple(d for _, d in DIL_PAIRS) == (1, 4, 16)
    t, k = x.shape
    half = tm // 2
    return pl.pallas_call(
        _prenorm_body,
        grid=(t // tm,),
        in_specs=[pl.BlockSpec((tm, k), lambda i: (i, 0)), pl.BlockSpec((1, k), lambda i: (0, 0))],
        out_specs=pl.BlockSpec((DIL_GROUPS, tm, k), lambda i: (0, i, 0)),
        out_shape=jax.ShapeDtypeStruct((DIL_GROUPS, t, k), BF16),
        scratch_shapes=[pltpu.VMEM((k // LANE, half, LANE), F32), pltpu.VMEM((k // LANE, half, LANE), F32)],
        compiler_params=_cparams("parallel"),
        name="prenorm_classes",
    )(x, gain.reshape(1, k))


def _mm_sig_body(h_ref, w_ref, o_ref, *, sig_from):
    j = pl.program_id(1)
    acc = jnp.dot(h_ref[...], w_ref[...], preferred_element_type=F32)

    @pl.when(j < sig_from)
    def _():
        o_ref[...] = acc.astype(o_ref.dtype)

    @pl.when(j >= sig_from)
    def _():
        o_ref[...] = _sigmoid(acc).astype(o_ref.dtype)


def matmul_sigmoid_tail(h3, w, *, tm, tn, sig_from):
    _, t, k = h3.shape
    n = w.shape[1]
    return pl.pallas_call(
        functools.partial(_mm_sig_body, sig_from=sig_from),
        grid=(t // tm, n // tn),
        in_specs=[
            pl.BlockSpec((None, tm, k), lambda i, j: (0, i, 0)),
            pl.BlockSpec((k, tn), lambda i, j: (0, j)),
        ],
        out_specs=pl.BlockSpec((tm, tn), lambda i, j: (i, j)),
        out_shape=jax.ShapeDtypeStruct((t, n), BF16),
        compiler_params=_cparams("parallel", "arbitrary"),
        name="mixer_in_proj",
    )(h3, w)


def _dil_proj_body(h_ref, w_ref, o0_ref, o1_ref, o2_ref, *, tiles_per_group):
    tm = h_ref.shape[0]
    group = pl.program_id(1) // tiles_per_group
    acc = jnp.dot(h_ref[...], w_ref[...], preferred_element_type=F32)
    for g, o_ref in enumerate((o0_ref, o1_ref, o2_ref)):
        d = o_ref.shape[0]
        cs = tm // d

        @pl.when(group == g)
        def _(o_ref=o_ref, d=d, cs=cs):
            for r in range(d):
                o_ref[r] = acc[r * cs:(r + 1) * cs, :].astype(o_ref.dtype)


def dilated_qkv_proj(h3, w, *, batch, seq, tm, tn):
    _, t, k = h3.shape
    unit = DIL_HPG * DIL_DH
    upt = unit // tn
    tpg = 3 * upt
    per_seq = seq // tm

    def w_col(j):
        return ((j % tpg) // upt) * (DIL_GROUPS * upt) + (j // tpg) * upt + j % upt

    out_specs, out_shape = [], []
    for g, (_, d) in enumerate(DIL_PAIRS):
        out_specs.append(pl.BlockSpec(
            (None, d, tm // d, tn),
            lambda i, j, g=g: (i // per_seq, 0, i % per_seq, jnp.clip(j - g * tpg, 0, tpg - 1))))
        out_shape.append(jax.ShapeDtypeStruct((batch, d, seq // d, 3 * unit), BF16))
    return pl.pallas_call(
        functools.partial(_dil_proj_body, tiles_per_group=tpg),
        grid=(t // tm, DIL_GROUPS * tpg),
        in_specs=[
            pl.BlockSpec((None, tm, k), lambda i, j: (j // tpg, i, 0)),
            pl.BlockSpec((k, tn), lambda i, j: (0, w_col(j))),
        ],
        out_specs=out_specs,
        out_shape=out_shape,
        compiler_params=_cparams("parallel", "arbitrary"),
        name="dilated_qkv_proj",
    )(h3, w)


def _rope128(pe, c_ref, sa_ref, sb_ref):
    return (pe * c_ref[...]
            + pltpu.roll(pe, 96, 1) * sa_ref[...]
            + pltpu.roll(pe, 32, 1) * sb_ref[...])


def _q_proj_body(qa_ref, g_ref, w_ref, c_ref, sa_ref, sb_ref, o_ref, h_scr, *, scale):
    _norm_into(qa_ref, g_ref, h_scr)
    for h in range(MLA_HEADS):
        acc = jnp.dot(h_scr[...], w_ref[:, h * MLA_QK:(h + 1) * MLA_QK], preferred_element_type=F32)
        o_ref[:, h * MLA_QK:h * MLA_QK + LANE] = (acc[:, :LANE] * scale).astype(o_ref.dtype)
        pe = _rope128(acc[:, LANE:], c_ref, sa_ref, sb_ref)
        o_ref[:, h * MLA_QK + LANE:(h + 1) * MLA_QK] = pe.astype(o_ref.dtype)


def q_proj(z, gain, w, tabs, *, seq, tm, scale):
    t = z.shape[0]
    r = w.shape[0]
    n = w.shape[1]
    per_seq = seq // tm
    tab_spec = pl.BlockSpec((tm, LANE), lambda i: (i % per_seq, 0))
    return pl.pallas_call(
        functools.partial(_q_proj_body, scale=scale),
        grid=(t // tm,),
        in_specs=[
            pl.BlockSpec((tm, r), lambda i: (i, 0)),
            pl.BlockSpec((1, r), lambda i: (0, 0)),
            pl.BlockSpec((r, n), lambda i: (0, 0)),
            tab_spec, tab_spec, tab_spec,
        ],
        out_specs=pl.BlockSpec((tm, n), lambda i: (i, 0)),
        out_shape=jax.ShapeDtypeStruct((t, n), BF16),
        scratch_shapes=[pltpu.VMEM((tm, r), BF16)],
        compiler_params=_cparams("parallel"),
        name="mla_q_proj",
    )(z, gain.reshape(1, r), w, *tabs)


def _kv_proj_body(ckv_ref, kpe_ref, g_ref, w_ref, c_ref, sa_ref, sb_ref, k_ref, v_ref, h_scr):
    _norm_into(ckv_ref, g_ref, h_scr)
    pe = _rope128(kpe_ref[...].astype(F32), c_ref, sa_ref, sb_ref).astype(k_ref.dtype)
    nk = MLA_HEADS * MLA_NOPE
    for h in range(MLA_HEADS):
        kn = jnp.dot(h_scr[...], w_ref[:, h * MLA_NOPE:(h + 1) * MLA_NOPE], preferred_element_type=F32)
        k_ref[:, h * MLA_QK:h * MLA_QK + LANE] = kn.astype(k_ref.dtype)
        k_ref[:, h * MLA_QK + LANE:(h + 1) * MLA_QK] = pe
    v = jnp.dot(h_scr[...], w_ref[:, nk:], preferred_element_type=F32)
    v_ref[...] = v.astype(v_ref.dtype)


def kv_proj(z, gain, w, tabs, *, seq, tm):
    t = z.shape[0]
    r = w.shape[0]
    per_seq = seq // tm
    nv = MLA_HEADS * MLA_V
    tab_spec = pl.BlockSpec((tm, LANE), lambda i: (i % per_seq, 0))
    return pl.pallas_call(
        _kv_proj_body,
        grid=(t // tm,),
        in_specs=[
            pl.BlockSpec((tm, r), lambda i: (i, MLA_Q_RANK // r)),
            pl.BlockSpec((tm, LANE), lambda i: (i, (MLA_Q_RANK + r) // LANE)),
            pl.BlockSpec((1, r), lambda i: (0, 0)),
            pl.BlockSpec(w.shape, lambda i: (0, 0)),
            tab_spec, tab_spec, tab_spec,
        ],
        out_specs=[
            pl.BlockSpec((tm, MLA_HEADS * MLA_QK), lambda i: (i, 0)),
            pl.BlockSpec((tm, nv), lambda i: (i, 0)),
        ],
        out_shape=[
            jax.ShapeDtypeStruct((t, MLA_HEADS * MLA_QK), BF16),
            jax.ShapeDtypeStruct((t, nv), BF16),
        ],
        scratch_shapes=[pltpu.VMEM((tm, r), BF16)],
        compiler_params=_cparams("parallel"),
        name="mla_kv_proj",
    )(z, z, gain.reshape(1, r), w, *tabs)


def _dot_nt(a, b):
    return lax.dot_general(a, b, (((1,), (1,)), ((), ())), preferred_element_type=F32)


def _mla_attn_body(q_ref, k_ref, v_ref, o_ref, *, tq, tr):
    qi = pl.program_id(2)
    row = lax.broadcasted_iota(jnp.int32, (tr, tr), 0)
    col = lax.broadcasted_iota(jnp.int32, (tr, tr), 1)

    def run(first):
        rows = slice(first % tq, first % tq + tr)
        q = q_ref[rows, :]
        s_d = jnp.where(col <= row, _dot_nt(q, k_ref[first:first + tr, :]), NEG)
        m = jnp.max(s_d, axis=-1, keepdims=True)
        if first:
            s_o = _dot_nt(q, k_ref[:first, :])
            m = jnp.maximum(m, jnp.max(s_o, axis=-1, keepdims=True))
        p_d = jnp.exp2(s_d - m)
        l = jnp.sum(p_d, axis=-1, keepdims=True)
        o = jnp.dot(p_d.astype(v_ref.dtype), v_ref[first:first + tr, :], preferred_element_type=F32)
        if first:
            p_o = jnp.exp2(s_o - m)
            l = l + jnp.sum(p_o, axis=-1, keepdims=True)
            o = o + jnp.dot(p_o.astype(v_ref.dtype), v_ref[:first, :], preferred_element_type=F32)
        o_ref[rows, :] = (o / l).astype(o_ref.dtype)

    for n in range(k_ref.shape[0] // tq):
        @pl.when(qi == n)
        def _(n=n):
            for first in range(n * tq, (n + 1) * tq, tr):
                run(first)


def mla_attention(q, k, v, *, batch, seq, tq):
    t = batch * seq
    q3 = q.reshape(batch, seq, MLA_HEADS * MLA_QK)
    k3 = k.reshape(batch, seq, MLA_HEADS * MLA_QK)
    v3 = v.reshape(batch, seq, MLA_HEADS * MLA_V)
    out = pl.pallas_call(
        functools.partial(_mla_attn_body, tq=tq, tr=MLA_ROW_RUN),
        grid=(batch, MLA_HEADS, seq // tq),
        in_specs=[
            pl.BlockSpec((None, tq, MLA_QK), lambda b, h, i: (b, i, h)),
            pl.BlockSpec((None, seq, MLA_QK), lambda b, h, i: (b, 0, h)),
            pl.BlockSpec((None, seq, MLA_V), lambda b, h, i: (b, 0, h)),
        ],
        out_specs=pl.BlockSpec((None, tq, MLA_V), lambda b, h, i: (b, i, h)),
        out_shape=jax.ShapeDtypeStruct((batch, seq, MLA_HEADS * MLA_V), BF16),
        compiler_params=_cparams("parallel", "parallel", "arbitrary"),
        name="mla_attention",
    )(q3, k3, v3)
    return out.reshape(t, MLA_HEADS * MLA_V)


def _dil_attn_body(q_ref, kc_ref, vc_ref, b_ref, o_ref, lse_ref, kp_ref, vp_ref, s_scr, p_scr, m_scr,
                   *, scale, rc):
    blk = q_ref.shape[0]

    @pl.when(pl.program_id(2) == 0)
    def _():
        kp_ref[...] = jnp.zeros_like(kp_ref)
        vp_ref[...] = jnp.zeros_like(vp_ref)

    for h in range(DIL_HPG):
        sl = slice(h * DIL_DH, (h + 1) * DIL_DH)
        rows = slice(h * blk, (h + 1) * blk)
        q = q_ref[:, sl]
        s_scr[rows, :blk] = _dot_nt(q, kp_ref[:, sl]) * scale + b_ref[h, :, :blk]
        s_scr[rows, blk:] = _dot_nt(q, kc_ref[:, sl]) * scale + b_ref[h, :, blk:]
    for c in range(DIL_HPG * blk // rc):
        rows = slice(c * rc, (c + 1) * rc)
        s = s_scr[rows, :]
        m = jnp.max(s, axis=-1, keepdims=True)
        p_scr[rows, :] = jnp.exp(s - m).astype(p_scr.dtype)
        m_scr[rows, :] = jnp.broadcast_to(m, (rc, LANE))
    ones = jnp.ones((2 * blk, LANE), p_scr.dtype)
    lane = lax.broadcasted_iota(jnp.int32, (blk, LANE), 1)
    lse_tile = jnp.zeros((blk, LANE), F32)
    for h in range(DIL_HPG):
        sl = slice(h * DIL_DH, (h + 1) * DIL_DH)
        rows = slice(h * blk, (h + 1) * blk)
        o = (jnp.dot(p_scr[rows, :blk], vp_ref[:, sl], preferred_element_type=F32)
             + jnp.dot(p_scr[rows, blk:], vc_ref[:, sl], preferred_element_type=F32))
        l = jnp.dot(p_scr[rows, :], ones, preferred_element_type=F32)
        o_ref[:, sl] = (o / l).astype(o_ref.dtype)
        lse_tile = jnp.where(lane == h, m_scr[rows, :] + jnp.log(l), lse_tile)
    lse_ref[...] = lse_tile
    kp_ref[...] = kc_ref[...]
    vp_ref[...] = vc_ref[...]


def dilated_group_attention(qkv, bias, *, group):
    batch, dilation, ln, _ = qkv.shape
    hw = DIL_HPG * DIL_DH
    nb = ln // DIL_BLOCK

    def spec(col):
        return pl.BlockSpec((None, None, DIL_BLOCK, hw), lambda b, r, n: (b, r, n, col))

    return pl.pallas_call(
        functools.partial(_dil_attn_body, scale=DIL_DH ** -0.5, rc=DIL_ROW_CHUNK),
        grid=(batch, dilation, nb),
        in_specs=[
            spec(0), spec(1), spec(2),
            pl.BlockSpec((None, DIL_HPG, DIL_BLOCK, 2 * DIL_BLOCK), lambda b, r, n: (jnp.minimum(n, 1), 0, 0, 0)),
        ],
        out_specs=[
            pl.BlockSpec((None, None, DIL_BLOCK, hw), lambda b, r, n: (b, r, n, 0)),
            pl.BlockSpec((None, None, DIL_BLOCK, LANE), lambda b, r, n: (b, r, n, 0)),
        ],
        out_shape=[
            jax.ShapeDtypeStruct((batch, dilation, ln, hw), BF16),
            jax.ShapeDtypeStruct((batch, dilation, ln, LANE), F32),
        ],
        scratch_shapes=[
            pltpu.VMEM((DIL_BLOCK, hw), qkv.dtype),
            pltpu.VMEM((DIL_BLOCK, hw), qkv.dtype),
            pltpu.VMEM((DIL_HPG * DIL_BLOCK, 2 * DIL_BLOCK), F32),
            pltpu.VMEM((DIL_HPG * DIL_BLOCK, 2 * DIL_BLOCK), BF16),
            pltpu.VMEM((DIL_HPG * DIL_BLOCK, LANE), F32),
        ],
        compiler_params=_cparams("arbitrary", "arbitrary", "arbitrary"),
        name=f"dilated_attention_g{group}",
    )(qkv, qkv, qkv, bias)


def _to_token_order(src_ref, dst_ref):
    d, cs, w = src_ref.shape
    for r in range(d):
        for c in range(w // LANE):
            dst_ref[c, pl.ds(r, cs, stride=d), :] = src_ref[r, :, c * LANE:(c + 1) * LANE].astype(dst_ref.dtype)


def _dil_combine_body(o0_ref, o1_ref, o2_ref, l0_ref, l1_ref, l2_ref, out_ref, o_scr, l_scr):
    for g, (o_ref, l_ref) in enumerate(((o1_ref, l1_ref), (o2_ref, l2_ref))):
        _to_token_order(o_ref, o_scr.at[g])
        _to_token_order(l_ref, l_scr.at[g])
    for h in range(DIL_HPG):
        sl = slice(h * DIL_DH, (h + 1) * DIL_DH)
        ls = [l0_ref[0, :, h:h + 1], l_scr[0, 0, :, h:h + 1], l_scr[1, 0, :, h:h + 1]]
        os_ = [o0_ref[0, :, sl].astype(F32), o_scr[0, h], o_scr[1, h]]
        m = jnp.maximum(jnp.maximum(ls[0], ls[1]), ls[2])
        es = [jnp.exp(x - m) for x in ls]
        den = es[0] + es[1] + es[2]
        acc = sum((e / den) * o for e, o in zip(es, os_))
        out_ref[:, sl] = acc.astype(out_ref.dtype)


def dilated_combine(os_, lses, *, seq, tm):
    batch = os_[0].shape[0]
    hw = os_[0].shape[-1]
    t = batch * seq
    per_seq = seq // tm

    def spec(arr):
        d, w = arr.shape[1], arr.shape[3]
        return pl.BlockSpec((None, d, tm // d, w), lambda i: (i // per_seq, 0, i % per_seq, 0))

    return pl.pallas_call(
        _dil_combine_body,
        grid=(t // tm,),
        in_specs=[spec(a) for a in (*os_, *lses)],
        out_specs=pl.BlockSpec((tm, hw), lambda i: (i, 0)),
        out_shape=jax.ShapeDtypeStruct((t, hw), BF16),
        scratch_shapes=[
            pltpu.VMEM((DIL_GROUPS - 1, hw // LANE, tm, LANE), F32),
            pltpu.VMEM((DIL_GROUPS - 1, 1, tm, LANE), F32),
        ],
        compiler_params=_cparams("parallel"),
        name="dilated_combine",
    )(*os_, *lses)


def _t5_bucket_np(dist):
    max_exact = REL_BUCKETS // 2
    large = max_exact + (np.log(np.maximum(dist, 1).astype(np.float32) / max_exact)
                         / math.log(REL_MAX_DIST / max_exact) * (REL_BUCKETS - max_exact)).astype(np.int32)
    large = np.minimum(large, REL_BUCKETS - 1)
    return np.where(dist < max_exact, dist, large)


def dilated_bias_tables(rel_bias):
    iq = np.arange(DIL_BLOCK)[:, None]
    jk = np.arange(2 * DIL_BLOCK)[None, :]
    dist_c = DIL_BLOCK + iq - jk
    tables = []
    for g, (window, dilation) in enumerate(DIL_PAIRS):
        span = window // dilation
        band = (dist_c >= 0) & (dist_c <= span)
        bucket = _t5_bucket_np(np.clip(dist_c, 0, None) * dilation)
        onehot = (bucket.reshape(-1, 1) == np.arange(REL_BUCKETS)[None, :]).astype(np.float32)
        heads = rel_bias[:, g * DIL_HPG:(g + 1) * DIL_HPG].astype(F32)
        bias = jnp.dot(onehot, heads, precision=lax.Precision.HIGHEST)
        bias = bias.reshape(DIL_BLOCK, 2 * DIL_BLOCK, DIL_HPG).transpose(2, 0, 1)
        general = jnp.where(band[None], bias, NEG)
        first = jnp.where((band & (jk >= DIL_BLOCK))[None], bias, NEG)
        tables.append(jnp.stack([first, general]))
    return tables


def _merge_body(a_ref, b_ref, wa_ref, wb_ref, ga_ref, gb_ref, o_ref):
    ya = jnp.dot(a_ref[...], wa_ref[...], preferred_element_type=F32)
    yb = jnp.dot(b_ref[...], wb_ref[...], preferred_element_type=F32)
    o_ref[...] = (ga_ref[...].astype(F32) * ya + gb_ref[...].astype(F32) * yb).astype(o_ref.dtype)


def merge_branches(o_a, o_b, w_a, w_b, z, *, gate_col, tm, tn):
    t, ka = o_a.shape
    kb = o_b.shape[1]
    n = w_a.shape[1]
    g0 = gate_col // tn
    return pl.pallas_call(
        _merge_body,
        grid=(t // tm, n // tn),
        in_specs=[
            pl.BlockSpec((tm, ka), lambda i, j: (i, 0)),
            pl.BlockSpec((tm, kb), lambda i, j: (i, 0)),
            pl.BlockSpec((ka, tn), lambda i, j: (0, j)),
            pl.BlockSpec((kb, tn), lambda i, j: (0, j)),
            pl.BlockSpec((tm, tn), lambda i, j: (i, g0 + j)),
            pl.BlockSpec((tm, tn), lambda i, j: (i, g0 + n // tn + j)),
        ],
        out_specs=pl.BlockSpec((tm, tn), lambda i, j: (i, j)),
        out_shape=jax.ShapeDtypeStruct((t, n), BF16),
        compiler_params=_cparams("parallel", "arbitrary"),
        name="merge_branches",
    )(o_a, o_b, w_a, w_b, z, z)


def _mm_residual_body(a_ref, w_ref, x_ref, o_ref):
    o_ref[...] = x_ref[...] + jnp.dot(a_ref[...], w_ref[...], preferred_element_type=F32)


def matmul_residual(a, w, x, *, tm, tn):
    t, k = a.shape
    n = w.shape[1]
    return pl.pallas_call(
        _mm_residual_body,
        grid=(t // tm, n // tn),
        in_specs=[
            pl.BlockSpec((tm, k), lambda i, j: (i, 0)),
            pl.BlockSpec((k, tn), lambda i, j: (0, j)),
            pl.BlockSpec((tm, tn), lambda i, j: (i, j)),
        ],
        out_specs=pl.BlockSpec((tm, tn), lambda i, j: (i, j)),
        out_shape=jax.ShapeDtypeStruct((t, n), F32),
        compiler_params=_cparams("parallel", "arbitrary"),
        name="matmul_residual",
    )(a, w, x)


def _swiglu_up(h, wg, wu):
    g = jnp.dot(h, wg, preferred_element_type=F32)
    u = jnp.dot(h, wu, preferred_element_type=F32)
    return g * _sigmoid(g) * u


def _swiglu_up_body(x_ref, g_ref, wg_ref, wu_ref, o_ref, h_scr):
    @pl.when(pl.program_id(1) == 0)
    def _():
        _norm_into(x_ref, g_ref, h_scr)

    o_ref[...] = _swiglu_up(h_scr[...], wg_ref[...], wu_ref[...]).astype(o_ref.dtype)


def swiglu_up(x, gain, w_gate, w_up, *, tm, tn):
    t, k = x.shape
    n = w_gate.shape[1]
    w_spec = pl.BlockSpec((k, tn), lambda i, j: (0, j))
    return pl.pallas_call(
        _swiglu_up_body,
        grid=(t // tm, n // tn),
        in_specs=[
            pl.BlockSpec((tm, k), lambda i, j: (i, 0)),
            pl.BlockSpec((1, k), lambda i, j: (0, 0)),
            w_spec, w_spec,
        ],
        out_specs=pl.BlockSpec((tm, tn), lambda i, j: (i, j)),
        out_shape=jax.ShapeDtypeStruct((t, n), BF16),
        scratch_shapes=[pltpu.VMEM((tm, k), BF16)],
        compiler_params=_cparams("parallel", "arbitrary"),
        name="swiglu_up",
    )(x, gain.reshape(1, k), w_gate, w_up)


def _grouped_up_body(te_ref, nu_ref, first_ref, next_ref, x_ref, g_ref, wg_ref, wu_ref, o_ref, buf, h_scr, sem,
                     *, nj):
    i = pl.program_id(0)
    j = pl.program_id(1)
    tm = o_ref.shape[0]
    part = tm // nj
    slot = i % 2

    @pl.when((i == 0) & (j == 0))
    def _():
        _start_row_gather(first_ref, x_ref, buf.at[0], sem.at[0])

    @pl.when(i < nu_ref[0])
    def _():
        @pl.when(j == 0)
        def _():
            _wait_row_gather(x_ref, buf.at[slot], sem.at[slot])
            _norm_into(buf.at[slot], g_ref, h_scr)

        _start_row_gather(next_ref, x_ref, buf.at[1 - slot], sem.at[1 - slot], first=j * part, count=part)
        o_ref[...] = _swiglu_up(h_scr[...], wg_ref[...], wu_ref[...]).astype(o_ref.dtype)

    @pl.when(i >= nu_ref[0])
    def _():
        @pl.when((i == nu_ref[0]) & (j == 0))
        def _():
            _wait_row_gather(x_ref, buf.at[slot], sem.at[slot])

        o_ref[...] = jnp.zeros_like(o_ref)


def grouped_swiglu_up(x, gain, row_token, w_gate, w_up, tile_expert, n_used, *, moe_layer, tm, tn):
    k = x.shape[1]
    r = row_token.shape[0]
    n = w_gate.shape[3]
    nj = n // tn
    n_tiles = r // tm
    idx3 = row_token.reshape(n_tiles, 1, tm)

    def col(i, j, nu):
        return jnp.where(i < nu[0], j, nj - 1)

    w_spec = pl.BlockSpec((None, None, k, tn), lambda i, j, te, nu: (moe_layer, te[i], 0, col(i, j, nu)))
    return pl.pallas_call(
        functools.partial(_grouped_up_body, nj=nj),
        grid_spec=pltpu.PrefetchScalarGridSpec(
            num_scalar_prefetch=2,
            grid=(n_tiles, nj),
            in_specs=[
                pl.BlockSpec((1, 1, tm), lambda i, j, te, nu: (0, 0, 0), memory_space=pltpu.SMEM),
                pl.BlockSpec((1, 1, tm), lambda i, j, te, nu: (jnp.minimum(i + 1, n_tiles - 1), 0, 0),
                             memory_space=pltpu.SMEM),
                pl.BlockSpec(memory_space=pl.ANY),
                pl.BlockSpec((1, k), lambda i, j, te, nu: (0, 0)),
                w_spec, w_spec,
            ],
            out_specs=pl.BlockSpec((tm, tn), lambda i, j, te, nu: (i, j)),
            scratch_shapes=[
                pltpu.VMEM((2, tm, k), x.dtype),
                pltpu.VMEM((tm, k), BF16),
                pltpu.SemaphoreType.DMA((2,)),
            ],
        ),
        out_shape=jax.ShapeDtypeStruct((r, n), BF16),
        compiler_params=_cparams("arbitrary", "arbitrary"),
        name="grouped_swiglu_up",
    )(tile_expert, n_used, idx3, idx3, x, gain.reshape(1, k), w_gate, w_up)


def _grouped_down_body(te_ref, nu_ref, a_ref, w_ref, o_ref):
    i = pl.program_id(0)

    @pl.when(i < nu_ref[0])
    def _():
        o_ref[...] = jnp.dot(a_ref[...], w_ref[...], preferred_element_type=F32)

    @pl.when(i >= nu_ref[0])
    def _():
        o_ref[...] = jnp.zeros_like(o_ref)


def grouped_down(a, w_down, tile_expert, n_used, *, moe_layer, tm, tn):
    r, k = a.shape
    n = w_down.shape[3]
    nj = n // tn

    def row(i, nu):
        return jnp.minimum(i, nu[0] - 1)

    def col(i, j, nu):
        return jnp.where(i < nu[0], j, nj - 1)

    return pl.pallas_call(
        _grouped_down_body,
        grid_spec=pltpu.PrefetchScalarGridSpec(
            num_scalar_prefetch=2,
            grid=(r // tm, nj),
            in_specs=[
                pl.BlockSpec((tm, k), lambda i, j, te, nu: (row(i, nu), 0)),
                pl.BlockSpec((None, None, k, tn), lambda i, j, te, nu: (moe_layer, te[i], 0, col(i, j, nu))),
            ],
            out_specs=pl.BlockSpec((tm, tn), lambda i, j, te, nu: (i, j)),
        ),
        out_shape=jax.ShapeDtypeStruct((r, n), F32),
        compiler_params=_cparams("arbitrary", "arbitrary"),
        name="grouped_down",
    )(tile_expert, n_used, a, w_down)


def _router_body(x_ref, g_ref, wr_ref, idx_ref, wt_ref):
    h = _rms_rows(x_ref[...], g_ref[...])
    logits = lax.dot_general(wr_ref[...], h, (((1,), (1,)), ((), ())),
                             precision=lax.Precision.HIGHEST, preferred_element_type=F32)
    e = lax.broadcasted_iota(jnp.int32, logits.shape, 0)
    n_e = logits.shape[0]
    m1 = jnp.max(logits, axis=0, keepdims=True)
    i1 = jnp.min(jnp.where(logits == m1, e, n_e), axis=0, keepdims=True)
    rest = jnp.where(e == i1, -jnp.inf, logits)
    m2 = jnp.max(rest, axis=0, keepdims=True)
    i2 = jnp.min(jnp.where(rest == m2, e, n_e), axis=0, keepdims=True)
    t = jnp.exp(m2 - m1)
    idx_ref[0:1, :] = i1
    idx_ref[1:2, :] = i2
    wt_ref[0:1, :] = 1.0 / (1.0 + t)
    wt_ref[1:2, :] = t / (1.0 + t)


def moe_router(x, gain, w_router_t, *, tm):
    t, k = x.shape
    n_e = w_router_t.shape[0]
    out_spec = pl.BlockSpec((TOP_K, tm), lambda i: (0, i))
    return pl.pallas_call(
        _router_body,
        grid=(t // tm,),
        in_specs=[
            pl.BlockSpec((tm, k), lambda i: (i, 0)),
            pl.BlockSpec((1, k), lambda i: (0, 0)),
            pl.BlockSpec((n_e, k), lambda i: (0, 0)),
        ],
        out_specs=[out_spec, out_spec],
        out_shape=[jax.ShapeDtypeStruct((TOP_K, t), jnp.int32), jax.ShapeDtypeStruct((TOP_K, t), F32)],
        compiler_params=_cparams("parallel"),
        name="moe_router",
    )(x, gain.reshape(1, k), w_router_t)


def _start_row_gather(idx_ref, src_ref, dst_ref, sem, first=0, count=None):
    count = dst_ref.shape[0] if count is None else count
    for r in range(count):
        row = first + r
        pltpu.make_async_copy(src_ref.at[pl.ds(idx_ref[0, 0, row], 1), :], dst_ref.at[pl.ds(row, 1), :], sem).start()


def _wait_row_gather(src_ref, dst_ref, sem):
    pltpu.make_async_copy(src_ref.at[pl.ds(0, dst_ref.shape[0]), :], dst_ref, sem).wait()


def _step_slots():
    i = pl.program_id(0)
    return i, pl.num_programs(0), i % 2


def _moe_combine_body(p0_ref, p1_ref, n0_ref, n1_ref, y_ref, x_ref, w_ref, o_ref, buf, sem):
    i, n, slot = _step_slots()

    def start(a_ref, b_ref, s):
        _start_row_gather(a_ref, y_ref, buf.at[s, 0], sem.at[s, 0])
        _start_row_gather(b_ref, y_ref, buf.at[s, 1], sem.at[s, 1])

    @pl.when(i == 0)
    def _():
        start(p0_ref, p1_ref, 0)

    @pl.when(i + 1 < n)
    def _():
        start(n0_ref, n1_ref, 1 - slot)

    _wait_row_gather(y_ref, buf.at[slot, 0], sem.at[slot, 0])
    _wait_row_gather(y_ref, buf.at[slot, 1], sem.at[slot, 1])
    w = w_ref[...]
    o_ref[...] = x_ref[...] + w[:, 0:1] * buf[slot, 0] + w[:, 1:2] * buf[slot, 1]


def moe_combine(x, y, pos, wts, *, tm):
    t, d = x.shape
    steps = t // tm
    pos3 = pos.reshape(TOP_K * steps, 1, tm)

    def idx_spec(choice, ahead):
        return pl.BlockSpec((1, 1, tm), lambda i: (choice * steps + jnp.minimum(i + ahead, steps - 1), 0, 0),
                            memory_space=pltpu.SMEM)

    return pl.pallas_call(
        _moe_combine_body,
        grid=(steps,),
        in_specs=[
            idx_spec(0, 0), idx_spec(1, 0), idx_spec(0, 1), idx_spec(1, 1),
            pl.BlockSpec(memory_space=pl.ANY),
            pl.BlockSpec((tm, d), lambda i: (i, 0)),
            pl.BlockSpec((tm, TOP_K), lambda i: (i, 0)),
        ],
        out_specs=pl.BlockSpec((tm, d), lambda i: (i, 0)),
        out_shape=jax.ShapeDtypeStruct((t, d), F32),
        scratch_shapes=[pltpu.VMEM((2, TOP_K, tm, d), y.dtype), pltpu.SemaphoreType.DMA((2, TOP_K))],
        compiler_params=_cparams("arbitrary"),
        name="moe_combine",
    )(pos3, pos3, pos3, pos3, y, x, wts)


def moe_dispatch_plan(top_idx, *, tm):
    k, t = top_idx.shape
    n_rows = k * t + N_EXPERTS * tm
    n_tiles = n_rows // tm
    e = top_idx.reshape(-1)
    onehot = (e[:, None] == jnp.arange(N_EXPERTS, dtype=jnp.int32)[None, :]).astype(jnp.int32)
    csum = jnp.cumsum(onehot, axis=0)
    counts = csum[-1]
    rank = jnp.sum((csum - onehot) * onehot, axis=1)
    padded = ((counts + tm - 1) // tm) * tm
    ends = jnp.cumsum(padded)
    starts = ends - padded
    pos = jnp.sum(onehot * starts[None, :], axis=1) + rank
    token = jnp.tile(jnp.arange(t, dtype=jnp.int32), k)
    row_token = jnp.zeros((n_rows,), jnp.int32).at[pos].set(token)
    n_used = (ends[-1] // tm).astype(jnp.int32)
    tile_start = jnp.minimum(jnp.arange(n_tiles, dtype=jnp.int32), n_used - 1) * tm
    tile_expert = jnp.sum((tile_start[:, None] >= ends[None, :]).astype(jnp.int32), axis=1)
    return row_token, pos.astype(jnp.int32), tile_expert.astype(jnp.int32), n_used.reshape(1)


def moe_ffn(x, gain, w_router, w_gate, w_up, w_down, moe_layer):
    idx, wts = moe_router(x, gain, w_router.T, tm=TM_MOE)
    row_token, pos, tile_expert, n_used = moe_dispatch_plan(idx, tm=TM_MOE)
    a = grouped_swiglu_up(x, gain, row_token, w_gate, w_up, tile_expert, n_used,
                          moe_layer=moe_layer, tm=TM_MOE, tn=TN_FF)
    y = grouped_down(a, w_down, tile_expert, n_used, moe_layer=moe_layer, tm=TM_MOE, tn=TN_WIDE)
    return moe_combine(x, y, pos, wts.T, tm=GATHER_ROWS)


def _rmsnorm_body(x_ref, g_ref, o_ref):
    o_ref[...] = _rms_rows(x_ref[...], g_ref[...])


def rmsnorm(x, gain, *, tm):
    t, k = x.shape
    return pl.pallas_call(
        _rmsnorm_body,
        grid=(t // tm,),
        in_specs=[pl.BlockSpec((tm, k), lambda i: (i, 0)), pl.BlockSpec((1, k), lambda i: (0, 0))],
        out_specs=pl.BlockSpec((tm, k), lambda i: (i, 0)),
        out_shape=jax.ShapeDtypeStruct((t, k), F32),
        compiler_params=_cparams("parallel"),
        name="final_rmsnorm",
    )(x, gain.reshape(1, k))


def _rope_tables(seq, scale):
    half = MLA_ROPE // 2
    inv = ROPE_THETA ** (-jnp.arange(half, dtype=F32) / half)
    ang = jnp.arange(seq).astype(F32)[:, None] * inv[None, :]
    cos, sin = jnp.cos(ang) * scale, jnp.sin(ang) * scale
    z = jnp.zeros_like(cos)
    cat = lambda *p: jnp.concatenate(p, axis=1)
    return cat(cos, cos, z, z), cat(-sin, z, z, z), cat(z, sin, z, z)


def _split_w_in_body(w_ref, small_ref, dil_ref, *, off_dil, off_gate, step):
    gate_dst = off_dil + 2 * LANE - MLA_ROPE
    small_ref[:, :off_dil] = w_ref[:, :off_dil].astype(small_ref.dtype)
    small_ref[:, off_dil:gate_dst] = jnp.zeros((w_ref.shape[0], gate_dst - off_dil), small_ref.dtype)
    for c0 in range(0, w_ref.shape[1] - off_gate, step):
        small_ref[:, gate_dst + c0:gate_dst + c0 + step] = (
            w_ref[:, off_gate + c0:off_gate + c0 + step].astype(small_ref.dtype))
    for c0 in range(0, off_gate - off_dil, step):
        dil_ref[:, c0:c0 + step] = w_ref[:, off_dil + c0:off_dil + c0 + step].astype(dil_ref.dtype)


def split_w_in(w_in, layer, *, tk):
    _, k, n = w_in.shape
    off_dil = MLA_Q_RANK + MLA_KV_RANK + MLA_ROPE
    off_gate = off_dil + 3 * DIL_HEADS * DIL_DH
    n_small = off_dil + 2 * LANE - MLA_ROPE + n - off_gate
    return pl.pallas_call(
        functools.partial(_split_w_in_body, off_dil=off_dil, off_gate=off_gate, step=TN_WIDE),
        grid=(k // tk,),
        in_specs=[pl.BlockSpec((None, tk, n), lambda i: (layer, i, 0))],
        out_specs=[
            pl.BlockSpec((tk, n_small), lambda i: (i, 0)),
            pl.BlockSpec((tk, off_gate - off_dil), lambda i: (i, 0)),
        ],
        out_shape=[
            jax.ShapeDtypeStruct((k, n_small), BF16),
            jax.ShapeDtypeStruct((k, off_gate - off_dil), BF16),
        ],
        compiler_params=_cparams("parallel"),
        name="split_w_in",
    )(w_in)


def _layout_w_uq(w):
    r = w.shape[0]
    w = w.reshape(r, MLA_HEADS, MLA_NOPE + MLA_ROPE)
    w = jnp.pad(w, ((0, 0), (0, 0), (0, MLA_QK - MLA_NOPE - MLA_ROPE)))
    return w.reshape(r, MLA_HEADS * MLA_QK).astype(BF16)


def _layout_w_ukv(w):
    r = w.shape[0]
    w = w.reshape(r, MLA_HEADS, MLA_NOPE + MLA_V)
    k = w[:, :, :MLA_NOPE].reshape(r, MLA_HEADS * MLA_NOPE)
    v = w[:, :, MLA_NOPE:].reshape(r, MLA_HEADS * MLA_V)
    return jnp.concatenate([k, v], axis=1).astype(BF16)


def kernel(x, mix_norm, w_in, q_norm, w_uq, kv_norm, w_ukv, w_o_mla, w_o_dil, w_out, rel_bias,
           ffn_norm, w_ffn_gate, w_ffn_up, w_ffn_down, w_router, w_exp_gate, w_exp_up, w_exp_down,
           final_norm):
    batch, seq, d_model = x.shape
    depth = w_in.shape[0]
    t = batch * seq
    x = x.reshape(t, d_model)

    q_scale = (MLA_NOPE + MLA_ROPE) ** -0.5 * math.log2(math.e)
    q_tabs = _rope_tables(seq, q_scale)
    k_tabs = _rope_tables(seq, 1.0)
    bias_tabs = dilated_bias_tables(rel_bias)
    gate_col = MLA_Q_RANK + MLA_KV_RANK + 2 * LANE
    exp_gate, exp_up, exp_down = (w.astype(BF16) for w in (w_exp_gate, w_exp_up, w_exp_down))

    for layer in range(depth):
        w_small, w_dil = split_w_in(w_in, layer, tk=W_SPLIT_ROWS)
        h3 = prenorm_classes(x, mix_norm[layer], tm=TM)
        z = matmul_sigmoid_tail(h3, w_small, tm=TM, tn=TN_WIDE, sig_from=gate_col // TN_WIDE)
        qkvs = dilated_qkv_proj(h3, w_dil, batch=batch, seq=seq, tm=TM, tn=TN_WIDE)

        q = q_proj(z, q_norm[layer], _layout_w_uq(w_uq[layer]), q_tabs, seq=seq, tm=TQ, scale=q_scale)
        k, v = kv_proj(z, kv_norm[layer], _layout_w_ukv(w_ukv[layer]), k_tabs, seq=seq, tm=TQ)
        o_mla = mla_attention(q, k, v, batch=batch, seq=seq, tq=TQ_ATT)

        os_, lses = [], []
        for g in range(DIL_GROUPS):
            o_g, lse_g = dilated_group_attention(qkvs[g], bias_tabs[g], group=g)
            os_.append(o_g)
            lses.append(lse_g)
        o_dil = dilated_combine(os_, lses, seq=seq, tm=TM_COMBINE)

        merged = merge_branches(o_mla, o_dil, w_o_mla[layer].astype(BF16), w_o_dil[layer].astype(BF16),
                                z, gate_col=gate_col, tm=TM, tn=TN_WIDE)
        x = matmul_residual(merged, w_out[layer].astype(BF16), x, tm=TM_DOWN, tn=d_model)

        i = layer // 2
        if layer % 2 == 0:
            a = swiglu_up(x, ffn_norm[layer], w_ffn_gate[i].astype(BF16), w_ffn_up[i].astype(BF16),
                          tm=TM, tn=TN)
            x = matmul_residual(a, w_ffn_down[i].astype(BF16), x, tm=TM_DOWN, tn=TN_WIDE)
        else:
            x = moe_ffn(x, ffn_norm[layer], w_router[i], exp_gate, exp_up, exp_down, i)

    return rmsnorm(x, final_norm, tm=TM).reshape(batch, seq, d_model)
```

```python
import functools
import math

import numpy as np
import jax
import jax.numpy as jnp
from jax import lax
from jax.experimental import pallas as pl
from jax.experimental.pallas import tpu as pltpu

F32 = jnp.float32
BF16 = jnp.bfloat16

EPS = 1e-6
NEG = -1e30

MLA_HEADS = 16
MLA_Q_RANK = 512
MLA_KV_RANK = 256
MLA_NOPE = 128
MLA_ROPE = 64
MLA_V = 128
ROPE_THETA = 10000.0
DIL_PAIRS = ((128, 1), (512, 4), (2048, 16))
DIL_GROUPS = 3
DIL_HPG = 8
DIL_HEADS = DIL_GROUPS * DIL_HPG
DIL_DH = 128
DIL_BLOCK = 128
REL_BUCKETS = 32
REL_MAX_DIST = 2048
N_EXPERTS = 8
TOP_K = 2

LANE = 128
MLA_QK = 256
VMEM_LIMIT = 56 * 2**20

TM = 1024
TM_DOWN = 512
TN = 512
TN_WIDE = 1024
TN_FF = 1408
TM_MOE = 512
TQ = 512
TQ_ATT = 2048
MLA_ROW_RUN = 256
DIL_ROW_CHUNK = 64
TM_COMBINE = 512
GATHER_ROWS = 256
W_SPLIT_ROWS = 128
NORM_CHUNK = 256


def _cparams(*sem):
    return pltpu.CompilerParams(dimension_semantics=sem, vmem_limit_bytes=VMEM_LIMIT)


def _rms_rows(x, g):
    ms = jnp.mean(x * x, axis=-1, keepdims=True)
    return x * lax.rsqrt(ms + EPS) * g


def _norm_into(x_ref, g_ref, h_scr):
    rows = x_ref.shape[0]
    chunk = min(NORM_CHUNK, rows)
    for r in range(0, rows, chunk):
        x = x_ref[r:r + chunk, :].astype(F32)
        h_scr[r:r + chunk, :] = _rms_rows(x, g_ref[...]).astype(h_scr.dtype)


def _sigmoid(x):
    return 1.0 / (1.0 + jnp.exp(-x))


def _prenorm_body(x_ref, g_ref, o_ref, a_scr, b_scr):
    tm, k = x_ref.shape
    half = a_scr.shape[1]
    chunk = min(NORM_CHUNK, half)
    q4, q16 = half // 4, half // 16
    for s in range(tm // half):
        for r0 in range(0, half, chunk):
            rows = slice(s * half + r0, s * half + r0 + chunk)
            h = _rms_rows(x_ref[rows, :], g_ref[...])
            o_ref[0, rows, :] = h.astype(o_ref.dtype)
            for c in range(k // LANE):
                a_scr[c, r0:r0 + chunk, :] = h[:, c * LANE:(c + 1) * LANE]
        for c in range(k // LANE):
            lanes = slice(c * LANE, (c + 1) * LANE)
            for r in range(4):
                h4 = a_scr[c, pl.ds(r, q4, stride=4), :]
                b_scr[c, r * q4:(r + 1) * q4, :] = h4
                dst = r * (tm // 4) + s * q4
                o_ref[1, dst:dst + q4, lanes] = h4.astype(o_ref.dtype)
            for r in range(4):
                for r2 in range(4):
                    h16 = b_scr[c, pl.ds(r * q4 + r2, q16, stride=4), :]
                    dst = (r + 4 * r2) * (tm // 16) + s * q16
                    o_ref[2, dst:dst + q16, lanes] = h16.astype(o_ref.dtype)


def prenorm_classes(x, gain, *, tm):
    assert tuple(d for _, d in DIL_PAIRS) == (1, 4, 16)
    t, k = x.shape
    half = tm // 2
    return pl.pallas_call(
        _prenorm_body,
        grid=(t // tm,),
        in_specs=[pl.BlockSpec((tm, k), lambda i: (i, 0)), pl.BlockSpec((1, k), lambda i: (0, 0))],
        out_specs=pl.BlockSpec((DIL_GROUPS, tm, k), lambda i: (0, i, 0)),
        out_shape=jax.ShapeDtypeStruct((DIL_GROUPS, t, k), BF16),
        scratch_shapes=[pltpu.VMEM((k // LANE, half, LANE), F32), pltpu.VMEM((k // LANE, half, LANE), F32)],
        compiler_params=_cparams("parallel"),
        name="prenorm_classes",
    )(x, gain.reshape(1, k))


def _mm_sig_body(h_ref, w_ref, o_ref, *, sig_from):
    j = pl.program_id(1)
    acc = jnp.dot(h_ref[...], w_ref[...], preferred_element_type=F32)

    @pl.when(j < sig_from)
    def _():
        o_ref[...] = acc.astype(o_ref.dtype)

    @pl.when(j >= sig_from)
    def _():
        o_ref[...] = _sigmoid(acc).astype(o_ref.dtype)


def matmul_sigmoid_tail(h3, w, *, tm, tn, sig_from):
    _, t, k = h3.shape
    n = w.shape[1]
    return pl.pallas_call(
        functools.partial(_mm_sig_body, sig_from=sig_from),
        grid=(t // tm, n // tn),
        in_specs=[
            pl.BlockSpec((None, tm, k), lambda i, j: (0, i, 0)),
            pl.BlockSpec((k, tn), lambda i, j: (0, j)),
        ],
        out_specs=pl.BlockSpec((tm, tn), lambda i, j: (i, j)),
        out_shape=jax.ShapeDtypeStruct((t, n), BF16),
        compiler_params=_cparams("parallel", "arbitrary"),
        name="mixer_in_proj",
    )(h3, w)


def _dil_proj_body(h_ref, w_ref, o0_ref, o1_ref, o2_ref, *, tiles_per_group):
    tm = h_ref.shape[0]
    group = pl.program_id(1) // tiles_per_group
    acc = jnp.dot(h_ref[...], w_ref[...], preferred_element_type=F32)
    for g, o_ref in enumerate((o0_ref, o1_ref, o2_ref)):
        d = o_ref.shape[0]
        cs = tm // d

        @pl.when(group == g)
        def _(o_ref=o_ref, d=d, cs=cs):
            for r in range(d):
                o_ref[r] = acc[r * cs:(r + 1) * cs, :].astype(o_ref.dtype)


def dilated_qkv_proj(h3, w, *, batch, seq, tm, tn):
    _, t, k = h3.shape
    unit = DIL_HPG * DIL_DH
    upt = unit // tn
    tpg = 3 * upt
    per_seq = seq // tm

    def w_col(j):
        return ((j % tpg) // upt) * (DIL_GROUPS * upt) + (j // tpg) * upt + j % upt

    out_specs, out_shape = [], []
    for g, (_, d) in enumerate(DIL_PAIRS):
        out_specs.append(pl.BlockSpec(
            (None, d, tm // d, tn),
            lambda i, j, g=g: (i // per_seq, 0, i % per_seq, jnp.clip(j - g * tpg, 0, tpg - 1))))
        out_shape.append(jax.ShapeDtypeStruct((batch, d, seq // d, 3 * unit), BF16))
    return pl.pallas_call(
        functools.partial(_dil_proj_body, tiles_per_group=tpg),
        grid=(t // tm, DIL_GROUPS * tpg),
        in_specs=[
            pl.BlockSpec((None, tm, k), lambda i, j: (j // tpg, i, 0)),
            pl.BlockSpec((k, tn), lambda i, j: (0, w_col(j))),
        ],
        out_specs=out_specs,
        out_shape=out_shape,
        compiler_params=_cparams("parallel", "arbitrary"),
        name="dilated_qkv_proj",
    )(h3, w)


def _rope128(pe, c_ref, sa_ref, sb_ref):
    return (pe * c_ref[...]
            + pltpu.roll(pe, 96, 1) * sa_ref[...]
            + pltpu.roll(pe, 32, 1) * sb_ref[...])


def _q_proj_body(qa_ref, g_ref, w_ref, c_ref, sa_ref, sb_ref, o_ref, h_scr, *, scale):
    _norm_into(qa_ref, g_ref, h_scr)
    for h in range(MLA_HEADS):
        acc = jnp.dot(h_scr[...], w_ref[:, h * MLA_QK:(h + 1) * MLA_QK], preferred_element_type=F32)
        o_ref[:, h * MLA_QK:h * MLA_QK + LANE] = (acc[:, :LANE] * scale).astype(o_ref.dtype)
        pe = _rope128(acc[:, LANE:], c_ref, sa_ref, sb_ref)
        o_ref[:, h * MLA_QK + LANE:(h + 1) * MLA_QK] = pe.astype(o_ref.dtype)


def q_proj(z, gain, w, tabs, *, seq, tm, scale):
    t = z.shape[0]
    r = w.shape[0]
    n = w.shape[1]
    per_seq = seq // tm
    tab_spec = pl.BlockSpec((tm, LANE), lambda i: (i % per_seq, 0))
    return pl.pallas_call(
        functools.partial(_q_proj_body, scale=scale),
        grid=(t // tm,),
        in_specs=[
            pl.BlockSpec((tm, r), lambda i: (i, 0)),
            pl.BlockSpec((1, r), lambda i: (0, 0)),
            pl.BlockSpec((r, n), lambda i: (0, 0)),
            tab_spec, tab_spec, tab_spec,
        ],
        out_specs=pl.BlockSpec((tm, n), lambda i: (i, 0)),
        out_shape=jax.ShapeDtypeStruct((t, n), BF16),
        scratch_shapes=[pltpu.VMEM((tm, r), BF16)],
        compiler_params=_cparams("parallel"),
        name="mla_q_proj",
    )(z, gain.reshape(1, r), w, *tabs)


def _kv_proj_body(ckv_ref, kpe_ref, g_ref, w_ref, c_ref, sa_ref, sb_ref, k_ref, v_ref, h_scr):
    _norm_into(ckv_ref, g_ref, h_scr)
    pe = _rope128(kpe_ref[...].astype(F32), c_ref, sa_ref, sb_ref).astype(k_ref.dtype)
    nk = MLA_HEADS * MLA_NOPE
    for h in range(MLA_HEADS):
        kn = jnp.dot(h_scr[...], w_ref[:, h * MLA_NOPE:(h + 1) * MLA_NOPE], preferred_element_type=F32)
        k_ref[:, h * MLA_QK:h * MLA_QK + LANE] = kn.astype(k_ref.dtype)
        k_ref[:, h * MLA_QK + LANE:(h + 1) * MLA_QK] = pe
    v = jnp.dot(h_scr[...], w_ref[:, nk:], preferred_element_type=F32)
    v_ref[...] = v.astype(v_ref.dtype)


def kv_proj(z, gain, w, tabs, *, seq, tm):
    t = z.shape[0]
    r = w.shape[0]
    per_seq = seq // tm
    nv = MLA_HEADS * MLA_V
    tab_spec = pl.BlockSpec((tm, LANE), lambda i: (i % per_seq, 0))
    return pl.pallas_call(
        _kv_proj_body,
        grid=(t // tm,),
        in_specs=[
            pl.BlockSpec((tm, r), lambda i: (i, MLA_Q_RANK // r)),
            pl.BlockSpec((tm, LANE), lambda i: (i, (MLA_Q_RANK + r) // LANE)),
            pl.BlockSpec((1, r), lambda i: (0, 0)),
            pl.BlockSpec(w.shape, lambda i: (0, 0)),
            tab_spec, tab_spec, tab_spec,
        ],
        out_specs=[
            pl.BlockSpec((tm, MLA_HEADS * MLA_QK), lambda i: (i, 0)),
            pl.BlockSpec((tm, nv), lambda i: (i, 0)),
        ],
        out_shape=[
            jax.ShapeDtypeStruct((t, MLA_HEADS * MLA_QK), BF16),
            jax.ShapeDtypeStruct((t, nv), BF16),
        ],
        scratch_shapes=[pltpu.VMEM((tm, r), BF16)],
        compiler_params=_cparams("parallel"),
        name="mla_kv_proj",
    )(z, z, gain.reshape(1, r), w, *tabs)


def _dot_nt(a, b):
    return lax.dot_general(a, b, (((1,), (1,)), ((), ())), preferred_element_type=F32)


def _mla_attn_body(q_ref, k_ref, v_ref, o_ref, *, tq, tr):
    qi = pl.program_id(2)
    row = lax.broadcasted_iota(jnp.int32, (tr, tr), 0)
    col = lax.broadcasted_iota(jnp.int32, (tr, tr), 1)

    def run(first):
        rows = slice(first % tq, first % tq + tr)
        q = q_ref[rows, :]
        s_d = jnp.where(col <= row, _dot_nt(q, k_ref[first:first + tr, :]), NEG)
        m = jnp.max(s_d, axis=-1, keepdims=True)
        if first:
            s_o = _dot_nt(q, k_ref[:first, :])
            m = jnp.maximum(m, jnp.max(s_o, axis=-1, keepdims=True))
        p_d = jnp.exp2(s_d - m)
        l = jnp.sum(p_d, axis=-1, keepdims=True)
        o = jnp.dot(p_d.astype(v_ref.dtype), v_ref[first:first + tr, :], preferred_element_type=F32)
        if first:
            p_o = jnp.exp2(s_o - m)
            l = l + jnp.sum(p_o, axis=-1, keepdims=True)
            o = o + jnp.dot(p_o.astype(v_ref.dtype), v_ref[:first, :], preferred_element_type=F32)
        o_ref[rows, :] = (o / l).astype(o_ref.dtype)

    for n in range(k_ref.shape[0] // tq):
        @pl.when(qi == n)
        def _(n=n):
            for first in range(n * tq, (n + 1) * tq, tr):
                run(first)


def mla_attention(q, k, v, *, batch, seq, tq):
    t = batch * seq
    q3 = q.reshape(batch, seq, MLA_HEADS * MLA_QK)
    k3 = k.reshape(batch, seq, MLA_HEADS * MLA_QK)
    v3 = v.reshape(batch, seq, MLA_HEADS * MLA_V)
    out = pl.pallas_call(
        functools.partial(_mla_attn_body, tq=tq, tr=MLA_ROW_RUN),
        grid=(batch, MLA_HEADS, seq // tq),
        in_specs=[
            pl.BlockSpec((None, tq, MLA_QK), lambda b, h, i: (b, i, h)),
            pl.BlockSpec((None, seq, MLA_QK), lambda b, h, i: (b, 0, h)),
            pl.BlockSpec((None, seq, MLA_V), lambda b, h, i: (b, 0, h)),
        ],
        out_specs=pl.BlockSpec((None, tq, MLA_V), lambda b, h, i: (b, i, h)),
        out_shape=jax.ShapeDtypeStruct((batch, seq, MLA_HEADS * MLA_V), BF16),
        compiler_params=_cparams("parallel", "parallel", "arbitrary"),
        name="mla_attention",
    )(q3, k3, v3)
    return out.reshape(t, MLA_HEADS * MLA_V)


def _dil_attn_body(q_ref, kc_ref, vc_ref, b_ref, o_ref, lse_ref, kp_ref, vp_ref, s_scr, p_scr, m_scr,
                   *, scale, rc):
    blk = q_ref.shape[0]

    @pl.when(pl.program_id(2) == 0)
    def _():
        kp_ref[...] = jnp.zeros_like(kp_ref)
        vp_ref[...] = jnp.zeros_like(vp_ref)

    for h in range(DIL_HPG):
        sl = slice(h * DIL_DH, (h + 1) * DIL_DH)
        rows = slice(h * blk, (h + 1) * blk)
        q = q_ref[:, sl]
        s_scr[rows, :blk] = _dot_nt(q, kp_ref[:, sl]) * scale + b_ref[h, :, :blk]
        s_scr[rows, blk:] = _dot_nt(q, kc_ref[:, sl]) * scale + b_ref[h, :, blk:]
    for c in range(DIL_HPG * blk // rc):
        rows = slice(c * rc, (c + 1) * rc)
        s = s_scr[rows, :]
        m = jnp.max(s, axis=-1, keepdims=True)
        p_scr[rows, :] = jnp.exp(s - m).astype(p_scr.dtype)
        m_scr[rows, :] = jnp.broadcast_to(m, (rc, LANE))
    ones = jnp.ones((2 * blk, LANE), p_scr.dtype)
    lane = lax.broadcasted_iota(jnp.int32, (blk, LANE), 1)
    lse_tile = jnp.zeros((blk, LANE), F32)
    for h in range(DIL_HPG):
        sl = slice(h * DIL_DH, (h + 1) * DIL_DH)
        rows = slice(h * blk, (h + 1) * blk)
        o = (jnp.dot(p_scr[rows, :blk], vp_ref[:, sl], preferred_element_type=F32)
             + jnp.dot(p_scr[rows, blk:], vc_ref[:, sl], preferred_element_type=F32))
        l = jnp.dot(p_scr[rows, :], ones, preferred_element_type=F32)
        o_ref[:, sl] = (o / l).astype(o_ref.dtype)
        lse_tile = jnp.where(lane == h, m_scr[rows, :] + jnp.log(l), lse_tile)
    lse_ref[...] = lse_tile
    kp_ref[...] = kc_ref[...]
    vp_ref[...] = vc_ref[...]


def dilated_group_attention(qkv, bias, *, group):
    batch, dilation, ln, _ = qkv.shape
    hw = DIL_HPG * DIL_DH
    nb = ln // DIL_BLOCK

    def spec(col):
        return pl.BlockSpec((None, None, DIL_BLOCK, hw), lambda b, r, n: (b, r, n, col))

    return pl.pallas_call(
        functools.partial(_dil_attn_body, scale=DIL_DH ** -0.5, rc=DIL_ROW_CHUNK),
        grid=(batch, dilation, nb),
        in_specs=[
            spec(0), spec(1), spec(2),
            pl.BlockSpec((None, DIL_HPG, DIL_BLOCK, 2 * DIL_BLOCK), lambda b, r, n: (jnp.minimum(n, 1), 0, 0, 0)),
        ],
        out_specs=[
            pl.BlockSpec((None, None, DIL_BLOCK, hw), lambda b, r, n: (b, r, n, 0)),
            pl.BlockSpec((None, None, DIL_BLOCK, LANE), lambda b, r, n: (b, r, n, 0)),
        ],
        out_shape=[
            jax.ShapeDtypeStruct((batch, dilation, ln, hw), BF16),
            jax.ShapeDtypeStruct((batch, dilation, ln, LANE), F32),
        ],
        scratch_shapes=[
            pltpu.VMEM((DIL_BLOCK, hw), qkv.dtype),
            pltpu.VMEM((DIL_BLOCK, hw), qkv.dtype),
            pltpu.VMEM((DIL_HPG * DIL_BLOCK, 2 * DIL_BLOCK), F32),
            pltpu.VMEM((DIL_HPG * DIL_BLOCK, 2 * DIL_BLOCK), BF16),
            pltpu.VMEM((DIL_HPG * DIL_BLOCK, LANE), F32),
        ],
        compiler_params=_cparams("arbitrary", "arbitrary", "arbitrary"),
        name=f"dilated_attention_g{group}",
    )(qkv, qkv, qkv, bias)


def _to_token_order(src_ref, dst_ref):
    d, cs, w = src_ref.shape
    for r in range(d):
        for c in range(w // LANE):
            dst_ref[c, pl.ds(r, cs, stride=d), :] = src_ref[r, :, c * LANE:(c + 1) * LANE].astype(dst_ref.dtype)


def _dil_combine_body(o0_ref, o1_ref, o2_ref, l0_ref, l1_ref, l2_ref, out_ref, o_scr, l_scr):
    for g, (o_ref, l_ref) in enumerate(((o1_ref, l1_ref), (o2_ref, l2_ref))):
        _to_token_order(o_ref, o_scr.at[g])
        _to_token_order(l_ref, l_scr.at[g])
    for h in range(DIL_HPG):
        sl = slice(h * DIL_DH, (h + 1) * DIL_DH)
        ls = [l0_ref[0, :, h:h + 1], l_scr[0, 0, :, h:h + 1], l_scr[1, 0, :, h:h + 1]]
        os_ = [o0_ref[0, :, sl].astype(F32), o_scr[0, h], o_scr[1, h]]
        m = jnp.maximum(jnp.maximum(ls[0], ls[1]), ls[2])
        es = [jnp.exp(x - m) for x in ls]
        den = es[0] + es[1] + es[2]
        acc = sum((e / den) * o for e, o in zip(es, os_))
        out_ref[:, sl] = acc.astype(out_ref.dtype)


def dilated_combine(os_, lses, *, seq, tm):
    batch = os_[0].shape[0]
    hw = os_[0].shape[-1]
    t = batch * seq
    per_seq = seq // tm

    def spec(arr):
        d, w = arr.shape[1], arr.shape[3]
        return pl.BlockSpec((None, d, tm // d, w), lambda i: (i // per_seq, 0, i % per_seq, 0))

    return pl.pallas_call(
        _dil_combine_body,
        grid=(t // tm,),
        in_specs=[spec(a) for a in (*os_, *lses)],
        out_specs=pl.BlockSpec((tm, hw), lambda i: (i, 0)),
        out_shape=jax.ShapeDtypeStruct((t, hw), BF16),
        scratch_shapes=[
            pltpu.VMEM((DIL_GROUPS - 1, hw // LANE, tm, LANE), F32),
            pltpu.VMEM((DIL_GROUPS - 1, 1, tm, LANE), F32),
        ],
        compiler_params=_cparams("parallel"),
        name="dilated_combine",
    )(*os_, *lses)


def _t5_bucket_np(dist):
    max_exact = REL_BUCKETS // 2
    large = max_exact + (np.log(np.maximum(dist, 1).astype(np.float32) / max_exact)
                         / math.log(REL_MAX_DIST / max_exact) * (REL_BUCKETS - max_exact)).astype(np.int32)
    large = np.minimum(large, REL_BUCKETS - 1)
    return np.where(dist < max_exact, dist, large)


def dilated_bias_tables(rel_bias):
    iq = np.arange(DIL_BLOCK)[:, None]
    jk = np.arange(2 * DIL_BLOCK)[None, :]
    dist_c = DIL_BLOCK + iq - jk
    tables = []
    for g, (window, dilation) in enumerate(DIL_PAIRS):
        span = window // dilation
        band = (dist_c >= 0) & (dist_c <= span)
        bucket = _t5_bucket_np(np.clip(dist_c, 0, None) * dilation)
        onehot = (bucket.reshape(-1, 1) == np.arange(REL_BUCKETS)[None, :]).astype(np.float32)
        heads = rel_bias[:, g * DIL_HPG:(g + 1) * DIL_HPG].astype(F32)
        bias = jnp.dot(onehot, heads, precision=lax.Precision.HIGHEST)
        bias = bias.reshape(DIL_BLOCK, 2 * DIL_BLOCK, DIL_HPG).transpose(2, 0, 1)
        general = jnp.where(band[None], bias, NEG)
        first = jnp.where((band & (jk >= DIL_BLOCK))[None], bias, NEG)
        tables.append(jnp.stack([first, general]))
    return tables


def _merge_body(a_ref, b_ref, wa_ref, wb_ref, ga_ref, gb_ref, o_ref):
    ya = jnp.dot(a_ref[...], wa_ref[...], preferred_element_type=F32)
    yb = jnp.dot(b_ref[...], wb_ref[...], preferred_element_type=F32)
    o_ref[...] = (ga_ref[...].astype(F32) * ya + gb_ref[...].astype(F32) * yb).astype(o_ref.dtype)


def merge_branches(o_a, o_b, w_a, w_b, z, *, gate_col, tm, tn):
    t, ka = o_a.shape
    kb = o_b.shape[1]
    n = w_a.shape[1]
    g0 = gate_col // tn
    return pl.pallas_call(
        _merge_body,
        grid=(t // tm, n // tn),
        in_specs=[
            pl.BlockSpec((tm, ka), lambda i, j: (i, 0)),
            pl.BlockSpec((tm, kb), lambda i, j: (i, 0)),
            pl.BlockSpec((ka, tn), lambda i, j: (0, j)),
            pl.BlockSpec((kb, tn), lambda i, j: (0, j)),
            pl.BlockSpec((tm, tn), lambda i, j: (i, g0 + j)),
            pl.BlockSpec((tm, tn), lambda i, j: (i, g0 + n // tn + j)),
        ],
        out_specs=pl.BlockSpec((tm, tn), lambda i, j: (i, j)),
        out_shape=jax.ShapeDtypeStruct((t, n), BF16),
        compiler_params=_cparams("parallel", "arbitrary"),
        name="merge_branches",
    )(o_a, o_b, w_a, w_b, z, z)


def _mm_residual_body(a_ref, w_ref, x_ref, o_ref):
    o_ref[...] = x_ref[...] + jnp.dot(a_ref[...], w_ref[...], preferred_element_type=F32)


def matmul_residual(a, w, x, *, tm, tn):
    t, k = a.shape
    n = w.shape[1]
    return pl.pallas_call(
        _mm_residual_body,
        grid=(t // tm, n // tn),
        in_specs=[
            pl.BlockSpec((tm, k), lambda i, j: (i, 0)),
            pl.BlockSpec((k, tn), lambda i, j: (0, j)),
            pl.BlockSpec((tm, tn), lambda i, j: (i, j)),
        ],
        out_specs=pl.BlockSpec((tm, tn), lambda i, j: (i, j)),
        out_shape=jax.ShapeDtypeStruct((t, n), F32),
        compiler_params=_cparams("parallel", "arbitrary"),
        name="matmul_residual",
    )(a, w, x)


def _swiglu_up(h, wg, wu):
    g = jnp.dot(h, wg, preferred_element_type=F32)
    u = jnp.dot(h, wu, preferred_element_type=F32)
    return g * _sigmoid(g) * u


def _swiglu_up_body(x_ref, g_ref, wg_ref, wu_ref, o_ref, h_scr):
    @pl.when(pl.program_id(1) == 0)
    def _():
        _norm_into(x_ref, g_ref, h_scr)

    o_ref[...] = _swiglu_up(h_scr[...], wg_ref[...], wu_ref[...]).astype(o_ref.dtype)


def swiglu_up(x, gain, w_gate, w_up, *, tm, tn):
    t, k = x.shape
    n = w_gate.shape[1]
    w_spec = pl.BlockSpec((k, tn), lambda i, j: (0, j))
    return pl.pallas_call(
        _swiglu_up_body,
        grid=(t // tm, n // tn),
        in_specs=[
            pl.BlockSpec((tm, k), lambda i, j: (i, 0)),
            pl.BlockSpec((1, k), lambda i, j: (0, 0)),
            w_spec, w_spec,
        ],
        out_specs=pl.BlockSpec((tm, tn), lambda i, j: (i, j)),
        out_shape=jax.ShapeDtypeStruct((t, n), BF16),
        scratch_shapes=[pltpu.VMEM((tm, k), BF16)],
        compiler_params=_cparams("parallel", "arbitrary"),
        name="swiglu_up",
    )(x, gain.reshape(1, k), w_gate, w_up)


def _grouped_up_body(te_ref, nu_ref, first_ref, next_ref, x_ref, g_ref, wg_ref, wu_ref, o_ref, buf, h_scr, sem,
                     *, nj):
    i = pl.program_id(0)
    j = pl.program_id(1)
    tm = o_ref.shape[0]
    part = tm // nj
    slot = i % 2

    @pl.when((i == 0) & (j == 0))
    def _():
        _start_row_gather(first_ref, x_ref, buf.at[0], sem.at[0])

    @pl.when(i < nu_ref[0])
    def _():
        @pl.when(j == 0)
        def _():
            _wait_row_gather(x_ref, buf.at[slot], sem.at[slot])
            _norm_into(buf.at[slot], g_ref, h_scr)

        _start_row_gather(next_ref, x_ref, buf.at[1 - slot], sem.at[1 - slot], first=j * part, count=part)
        o_ref[...] = _swiglu_up(h_scr[...], wg_ref[...], wu_ref[...]).astype(o_ref.dtype)

    @pl.when(i >= nu_ref[0])
    def _():
        @pl.when((i == nu_ref[0]) & (j == 0))
        def _():
            _wait_row_gather(x_ref, buf.at[slot], sem.at[slot])

        o_ref[...] = jnp.zeros_like(o_ref)


def grouped_swiglu_up(x, gain, row_token, w_gate, w_up, tile_expert, n_used, *, moe_layer, tm, tn):
    k = x.shape[1]
    r = row_token.shape[0]
    n = w_gate.shape[3]
    nj = n // tn
    n_tiles = r // tm
    idx3 = row_token.reshape(n_tiles, 1, tm)

    def col(i, j, nu):
        return jnp.where(i < nu[0], j, nj - 1)

    w_spec = pl.BlockSpec((None, None, k, tn), lambda i, j, te, nu: (moe_layer, te[i], 0, col(i, j, nu)))
    return pl.pallas_call(
        functools.partial(_grouped_up_body, nj=nj),
        grid_spec=pltpu.PrefetchScalarGridSpec(
            num_scalar_prefetch=2,
            grid=(n_tiles, nj),
            in_specs=[
                pl.BlockSpec((1, 1, tm), lambda i, j, te, nu: (0, 0, 0), memory_space=pltpu.SMEM),
                pl.BlockSpec((1, 1, tm), lambda i, j, te, nu: (jnp.minimum(i + 1, n_tiles - 1), 0, 0),
                             memory_space=pltpu.SMEM),
                pl.BlockSpec(memory_space=pl.ANY),
                pl.BlockSpec((1, k), lambda i, j, te, nu: (0, 0)),
                w_spec, w_spec,
            ],
            out_specs=pl.BlockSpec((tm, tn), lambda i, j, te, nu: (i, j)),
            scratch_shapes=[
                pltpu.VMEM((2, tm, k), x.dtype),
                pltpu.VMEM((tm, k), BF16),
                pltpu.SemaphoreType.DMA((2,)),
            ],
        ),
        out_shape=jax.ShapeDtypeStruct((r, n), BF16),
        compiler_params=_cparams("arbitrary", "arbitrary"),
        name="grouped_swiglu_up",
    )(tile_expert, n_used, idx3, idx3, x, gain.reshape(1, k), w_gate, w_up)


def _grouped_down_body(te_ref, nu_ref, a_ref, w_ref, o_ref):
    i = pl.program_id(0)

    @pl.when(i < nu_ref[0])
    def _():
        o_ref[...] = jnp.dot(a_ref[...], w_ref[...], preferred_element_type=F32)

    @pl.when(i >= nu_ref[0])
    def _():
        o_ref[...] = jnp.zeros_like(o_ref)


def grouped_down(a, w_down, tile_expert, n_used, *, moe_layer, tm, tn):
    r, k = a.shape
    n = w_down.shape[3]
    nj = n // tn

    def row(i, nu):
        return jnp.minimum(i, nu[0] - 1)

    def col(i, j, nu):
        return jnp.where(i < nu[0], j, nj - 1)

    return pl.pallas_call(
        _grouped_down_body,
        grid_spec=pltpu.PrefetchScalarGridSpec(
            num_scalar_prefetch=2,
            grid=(r // tm, nj),
            in_specs=[
                pl.BlockSpec((tm, k), lambda i, j, te, nu: (row(i, nu), 0)),
                pl.BlockSpec((None, None, k, tn), lambda i, j, te, nu: (moe_layer, te[i], 0, col(i, j, nu))),
            ],
            out_specs=pl.BlockSpec((tm, tn), lambda i, j, te, nu: (i, j)),
        ),
        out_shape=jax.ShapeDtypeStruct((r, n), F32),
        compiler_params=_cparams("arbitrary", "arbitrary"),
        name="grouped_down",
    )(tile_expert, n_used, a, w_down)


def _router_body(x_ref, g_ref, wr_ref, idx_ref, wt_ref):
    h = _rms_rows(x_ref[...], g_ref[...])
    logits = lax.dot_general(wr_ref[...], h, (((1,), (1,)), ((), ())),
                             precision=lax.Precision.HIGHEST, preferred_element_type=F32)
    e = lax.broadcasted_iota(jnp.int32, logits.shape, 0)
    n_e = logits.shape[0]
    m1 = jnp.max(logits, axis=0, keepdims=True)
    i1 = jnp.min(jnp.where(logits == m1, e, n_e), axis=0, keepdims=True)
    rest = jnp.where(e == i1, -jnp.inf, logits)
    m2 = jnp.max(rest, axis=0, keepdims=True)
    i2 = jnp.min(jnp.where(rest == m2, e, n_e), axis=0, keepdims=True)
    t = jnp.exp(m2 - m1)
    idx_ref[0:1, :] = i1
    idx_ref[1:2, :] = i2
    wt_ref[0:1, :] = 1.0 / (1.0 + t)
    wt_ref[1:2, :] = t / (1.0 + t)


def moe_router(x, gain, w_router_t, *, tm):
    t, k = x.shape
    n_e = w_router_t.shape[0]
    out_spec = pl.BlockSpec((TOP_K, tm), lambda i: (0, i))
    return pl.pallas_call(
        _router_body,
        grid=(t // tm,),
        in_specs=[
            pl.BlockSpec((tm, k), lambda i: (i, 0)),
            pl.BlockSpec((1, k), lambda i: (0, 0)),
            pl.BlockSpec((n_e, k), lambda i: (0, 0)),
        ],
        out_specs=[out_spec, out_spec],
        out_shape=[jax.ShapeDtypeStruct((TOP_K, t), jnp.int32), jax.ShapeDtypeStruct((TOP_K, t), F32)],
        compiler_params=_cparams("parallel"),
        name="moe_router",
    )(x, gain.reshape(1, k), w_router_t)


def _start_row_gather(idx_ref, src_ref, dst_ref, sem, first=0, count=None):
    count = dst_ref.shape[0] if count is None else count
    for r in range(count):
        row = first + r
        pltpu.make_async_copy(src_ref.at[pl.ds(idx_ref[0, 0, row], 1), :], dst_ref.at[pl.ds(row, 1), :],
                              sem).start(priority=r % 2)


def _wait_row_gather(src_ref, dst_ref, sem):
    pltpu.make_async_copy(src_ref.at[pl.ds(0, dst_ref.shape[0]), :], dst_ref, sem).wait()


def _step_slots():
    i = pl.program_id(0)
    return i, pl.num_programs(0), i % 2


def _moe_combine_body(p0_ref, p1_ref, n0_ref, n1_ref, y_ref, x_ref, w_ref, o_ref, buf, sem):
    i, n, slot = _step_slots()

    def start(a_ref, b_ref, s):
        _start_row_gather(a_ref, y_ref, buf.at[s, 0], sem.at[s, 0])
        _start_row_gather(b_ref, y_ref, buf.at[s, 1], sem.at[s, 1])

    @pl.when(i == 0)
    def _():
        start(p0_ref, p1_ref, 0)

    @pl.when(i + 1 < n)
    def _():
        start(n0_ref, n1_ref, 1 - slot)

    _wait_row_gather(y_ref, buf.at[slot, 0], sem.at[slot, 0])
    _wait_row_gather(y_ref, buf.at[slot, 1], sem.at[slot, 1])
    w = w_ref[...]
    o_ref[...] = x_ref[...] + w[:, 0:1] * buf[slot, 0] + w[:, 1:2] * buf[slot, 1]


def moe_combine(x, y, pos, wts, *, tm):
    t, d = x.shape
    steps = t // tm
    pos3 = pos.reshape(TOP_K * steps, 1, tm)

    def idx_spec(choice, ahead):
        return pl.BlockSpec((1, 1, tm), lambda i: (choice * steps + jnp.minimum(i + ahead, steps - 1), 0, 0),
                            memory_space=pltpu.SMEM)

    return pl.pallas_call(
        _moe_combine_body,
        grid=(steps,),
        in_specs=[
            idx_spec(0, 0), idx_spec(1, 0), idx_spec(0, 1), idx_spec(1, 1),
            pl.BlockSpec(memory_space=pl.ANY),
            pl.BlockSpec((tm, d), lambda i: (i, 0)),
            pl.BlockSpec((tm, TOP_K), lambda i: (i, 0)),
        ],
        out_specs=pl.BlockSpec((tm, d), lambda i: (i, 0)),
        out_shape=jax.ShapeDtypeStruct((t, d), F32),
        scratch_shapes=[pltpu.VMEM((2, TOP_K, tm, d), y.dtype), pltpu.SemaphoreType.DMA((2, TOP_K))],
        compiler_params=_cparams("arbitrary"),
        name="moe_combine",
    )(pos3, pos3, pos3, pos3, y, x, wts)


def moe_dispatch_plan(top_idx, *, tm):
    k, t = top_idx.shape
    n_rows = k * t + N_EXPERTS * tm
    n_tiles = n_rows // tm
    e = top_idx.reshape(-1)
    onehot = (e[:, None] == jnp.arange(N_EXPERTS, dtype=jnp.int32)[None, :]).astype(jnp.int32)
    csum = jnp.cumsum(onehot, axis=0)
    counts = csum[-1]
    rank = jnp.sum((csum - onehot) * onehot, axis=1)
    padded = ((counts + tm - 1) // tm) * tm
    ends = jnp.cumsum(padded)
    starts = ends - padded
    pos = jnp.sum(onehot * starts[None, :], axis=1) + rank
    token = jnp.tile(jnp.arange(t, dtype=jnp.int32), k)
    row_token = jnp.zeros((n_rows,), jnp.int32).at[pos].set(token)
    n_used = (ends[-1] // tm).astype(jnp.int32)
    tile_start = jnp.minimum(jnp.arange(n_tiles, dtype=jnp.int32), n_used - 1) * tm
    tile_expert = jnp.sum((tile_start[:, None] >= ends[None, :]).astype(jnp.int32), axis=1)
    return row_token, pos.astype(jnp.int32), tile_expert.astype(jnp.int32), n_used.reshape(1)


def moe_ffn(x, gain, w_router, w_gate, w_up, w_down, moe_layer):
    idx, wts = moe_router(x, gain, w_router.T, tm=TM_MOE)
    row_token, pos, tile_expert, n_used = moe_dispatch_plan(idx, tm=TM_MOE)
    a = grouped_swiglu_up(x, gain, row_token, w_gate, w_up, tile_expert, n_used,
                          moe_layer=moe_layer, tm=TM_MOE, tn=TN_FF)
    y = grouped_down(a, w_down, tile_expert, n_used, moe_layer=moe_layer, tm=TM_MOE, tn=TN_WIDE)
    return moe_combine(x, y, pos, wts.T, tm=GATHER_ROWS)


def _rmsnorm_body(x_ref, g_ref, o_ref):
    o_ref[...] = _rms_rows(x_ref[...], g_ref[...])


def rmsnorm(x, gain, *, tm):
    t, k = x.shape
    return pl.pallas_call(
        _rmsnorm_body,
        grid=(t // tm,),
        in_specs=[pl.BlockSpec((tm, k), lambda i: (i, 0)), pl.BlockSpec((1, k), lambda i: (0, 0))],
        out_specs=pl.BlockSpec((tm, k), lambda i: (i, 0)),
        out_shape=jax.ShapeDtypeStruct((t, k), F32),
        compiler_params=_cparams("parallel"),
        name="final_rmsnorm",
    )(x, gain.reshape(1, k))


def _rope_tables(seq, scale):
    half = MLA_ROPE // 2
    inv = ROPE_THETA ** (-jnp.arange(half, dtype=F32) / half)
    ang = jnp.arange(seq).astype(F32)[:, None] * inv[None, :]
    cos, sin = jnp.cos(ang) * scale, jnp.sin(ang) * scale
    z = jnp.zeros_like(cos)
    cat = lambda *p: jnp.concatenate(p, axis=1)
    return cat(cos, cos, z, z), cat(-sin, z, z, z), cat(z, sin, z, z)


def _split_w_in_body(w_ref, small_ref, dil_ref, *, off_dil, off_gate, step):
    gate_dst = off_dil + 2 * LANE - MLA_ROPE
    small_ref[:, :off_dil] = w_ref[:, :off_dil].astype(small_ref.dtype)
    small_ref[:, off_dil:gate_dst] = jnp.zeros((w_ref.shape[0], gate_dst - off_dil), small_ref.dtype)
    for c0 in range(0, w_ref.shape[1] - off_gate, step):
        small_ref[:, gate_dst + c0:gate_dst + c0 + step] = (
            w_ref[:, off_gate + c0:off_gate + c0 + step].astype(small_ref.dtype))
    for c0 in range(0, off_gate - off_dil, step):
        dil_ref[:, c0:c0 + step] = w_ref[:, off_dil + c0:off_dil + c0 + step].astype(dil_ref.dtype)


def split_w_in(w_in, layer, *, tk):
    _, k, n = w_in.shape
    off_dil = MLA_Q_RANK + MLA_KV_RANK + MLA_ROPE
    off_gate = off_dil + 3 * DIL_HEADS * DIL_DH
    n_small = off_dil + 2 * LANE - MLA_ROPE + n - off_gate
    return pl.pallas_call(
        functools.partial(_split_w_in_body, off_dil=off_dil, off_gate=off_gate, step=TN_WIDE),
        grid=(k // tk,),
        in_specs=[pl.BlockSpec((None, tk, n), lambda i: (layer, i, 0))],
        out_specs=[
            pl.BlockSpec((tk, n_small), lambda i: (i, 0)),
            pl.BlockSpec((tk, off_gate - off_dil), lambda i: (i, 0)),
        ],
        out_shape=[
            jax.ShapeDtypeStruct((k, n_small), BF16),
            jax.ShapeDtypeStruct((k, off_gate - off_dil), BF16),
        ],
        compiler_params=_cparams("parallel"),
        name="split_w_in",
    )(w_in)


def _layout_w_uq(w):
    r = w.shape[0]
    w = w.reshape(r, MLA_HEADS, MLA_NOPE + MLA_ROPE)
    w = jnp.pad(w, ((0, 0), (0, 0), (0, MLA_QK - MLA_NOPE - MLA_ROPE)))
    return w.reshape(r, MLA_HEADS * MLA_QK).astype(BF16)


def _layout_w_ukv(w):
    r = w.shape[0]
    w = w.reshape(r, MLA_HEADS, MLA_NOPE + MLA_V)
    k = w[:, :, :MLA_NOPE].reshape(r, MLA_HEADS * MLA_NOPE)
    v = w[:, :, MLA_NOPE:].reshape(r, MLA_HEADS * MLA_V)
    return jnp.concatenate([k, v], axis=1).astype(BF16)


def kernel(x, mix_norm, w_in, q_norm, w_uq, kv_norm, w_ukv, w_o_mla, w_o_dil, w_out, rel_bias,
           ffn_norm, w_ffn_gate, w_ffn_up, w_ffn_down, w_router, w_exp_gate, w_exp_up, w_exp_down,
           final_norm):
    batch, seq, d_model = x.shape
    depth = w_in.shape[0]
    t = batch * seq
    x = x.reshape(t, d_model)

    q_scale = (MLA_NOPE + MLA_ROPE) ** -0.5 * math.log2(math.e)
    q_tabs = _rope_tables(seq, q_scale)
    k_tabs = _rope_tables(seq, 1.0)
    bias_tabs = dilated_bias_tables(rel_bias)
    gate_col = MLA_Q_RANK + MLA_KV_RANK + 2 * LANE
    exp_gate, exp_up, exp_down = (w.astype(BF16) for w in (w_exp_gate, w_exp_up, w_exp_down))

    for layer in range(depth):
        w_small, w_dil = split_w_in(w_in, layer, tk=W_SPLIT_ROWS)
        h3 = prenorm_classes(x, mix_norm[layer], tm=TM)
        z = matmul_sigmoid_tail(h3, w_small, tm=TM, tn=TN_WIDE, sig_from=gate_col // TN_WIDE)
        qkvs = dilated_qkv_proj(h3, w_dil, batch=batch, seq=seq, tm=TM, tn=TN_WIDE)

        q = q_proj(z, q_norm[layer], _layout_w_uq(w_uq[layer]), q_tabs, seq=seq, tm=TQ, scale=q_scale)
        k, v = kv_proj(z, kv_norm[layer], _layout_w_ukv(w_ukv[layer]), k_tabs, seq=seq, tm=TQ)
        o_mla = mla_attention(q, k, v, batch=batch, seq=seq, tq=TQ_ATT)

        os_, lses = [], []
        for g in range(DIL_GROUPS):
            o_g, lse_g = dilated_group_attention(qkvs[g], bias_tabs[g], group=g)
            os_.append(o_g)
            lses.append(lse_g)
        o_dil = dilated_combine(os_, lses, seq=seq, tm=TM_COMBINE)

        merged = merge_branches(o_mla, o_dil, w_o_mla[layer].astype(BF16), w_o_dil[layer].astype(BF16),
                                z, gate_col=gate_col, tm=TM, tn=TN_WIDE)
        x = matmul_residual(merged, w_out[layer].astype(BF16), x, tm=TM_DOWN, tn=d_model)

        i = layer // 2
        if layer % 2 == 0:
            a = swiglu_up(x, ffn_norm[layer], w_ffn_gate[i].astype(BF16), w_ffn_up[i].astype(BF16),
                          tm=TM, tn=TN)
            x = matmul_residual(a, w_ffn_down[i].astype(BF16), x, tm=TM_DOWN, tn=TN_WIDE)
        else:
            x = moe_ffn(x, ffn_norm[layer], w_router[i], exp_gate, exp_up, exp_down, i)

    return rmsnorm(x, final_norm, tm=TM).reshape(batch, seq, d_model)
```
